```python
import math
import jax
import jax.numpy as jnp
from jax import lax
import numpy as np

D_MODEL = 1024
BATCH = 8
SEQ = 2048
DEPTH = 2
DEC_BATCH = 128
DEC_SEQ = 4
PAST_LEN = 16384
PAGE_SIZE = 128

HEAD_DIM = 64
NORM_EPS = 1e-6
Q_BLOCK = 128

RWKV_HEADS = 4
RWKV_DIM = RWKV_HEADS * HEAD_DIM
DECAY_LORA = 64
ICLR_LORA = 64
GATE_LORA = 128
RWKV_COLS = 3 * RWKV_DIM + DECAY_LORA + ICLR_LORA + GATE_LORA
GN_EPS = 64e-5

NSA_HEADS = 8
NSA_GROUPS = 2
NSA_HPG = NSA_HEADS // NSA_GROUPS
NSA_DIM = NSA_HEADS * HEAD_DIM
NSA_KV_COLS = 2 * NSA_GROUPS * HEAD_DIM
NSA_BRANCHES = 3
CMP_LEN = 32
CMP_STRIDE = 16
CMP_HIDDEN = 128
SEL_BLOCK = 64
TOP_N = 16
WINDOW = 512
FORCED_SCORE = 1e9
NSA_SCALE = HEAD_DIM ** -0.5

NUM_BUCKETS = 32
MAX_EXACT = NUM_BUCKETS // 2
REL_MAX_DIST = 128

MLA_HEADS = 4
Q_LORA = 192
KV_LORA = 128
NOPE_DIM = 64
ROPE_DIM = 32
V_DIM = 64
MLA_DIM = MLA_HEADS * V_DIM
ROPE_THETA = 10000.0
MLA_SCALE = (NOPE_DIM + ROPE_DIM) ** -0.5

N_BRANCH = 3

N_GROUPS = 4
EXPERTS_PER_GROUP = 8
N_EXPERTS = N_GROUPS * EXPERTS_PER_GROUP
TOP_K_IN_GROUP = 2
EXPERT_FF = 256

POOL_EXTRA_DIV = 4

IN_SPLITS = (RWKV_COLS, NSA_DIM, NSA_KV_COLS, NSA_KV_COLS, NSA_KV_COLS,
             NSA_BRANCHES * NSA_HEADS, Q_LORA, KV_LORA + ROPE_DIM, N_BRANCH * D_MODEL)
IN_COLS = sum(IN_SPLITS)

kernel_name = 'hybrid_rwkv7_nsa_mla_hmoe_step'


def rms_norm(x, g):
    xf = x.astype(jnp.float32)
    y = xf * lax.rsqrt(jnp.mean(xf * xf, axis=-1, keepdims=True) + NORM_EPS)
    return (y * g.astype(jnp.float32)).astype(x.dtype)


def masked_softmax(logits, mask):
    z = jnp.where(mask, logits.astype(jnp.float32), -jnp.inf)
    m = jnp.max(z, axis=-1, keepdims=True)
    m = jnp.where(jnp.isfinite(m), m, 0.0)
    e = jnp.exp(z - m)
    return e / jnp.maximum(jnp.sum(e, axis=-1, keepdims=True), 1e-30)


def t5_bucket(dist):
    n = jnp.maximum(dist, 0)
    nf = jnp.maximum(n, 1).astype(jnp.float32)
    large = MAX_EXACT + (jnp.log(nf / MAX_EXACT) / math.log(REL_MAX_DIST / MAX_EXACT)
                         * (NUM_BUCKETS - MAX_EXACT)).astype(jnp.int32)
    return jnp.where(n < MAX_EXACT, n, jnp.minimum(large, NUM_BUCKETS - 1))


def head_bias(rel_bias, dist):
    b = rel_bias[t5_bucket(dist)].astype(jnp.float32)
    return jnp.transpose(b.reshape(dist.shape + (NSA_GROUPS, NSA_HPG)), (2, 3, 0, 1))


def rope(x, pos):
    half = x.shape[-1] // 2
    freq = ROPE_THETA ** (-jnp.arange(half, dtype=jnp.float32) / half)
    ang = pos.astype(jnp.float32)[:, None] * freq[None, :]
    cos = jnp.cos(ang)[None, :, None, :]
    sin = jnp.sin(ang)[None, :, None, :]
    xf = x.astype(jnp.float32)
    x1, x2 = xf[..., :half], xf[..., half:]
    return jnp.concatenate([x1 * cos - x2 * sin, x1 * sin + x2 * cos], axis=-1).astype(x.dtype)


def gather_pages(pool, page_table):
    rows = pool[page_table]
    return rows.reshape((page_table.shape[0], page_table.shape[1] * pool.shape[1]) + pool.shape[2:])


def mixer_inputs(x, p):
    xn = rms_norm(x, p['norm_mix'])
    z = xn @ p['w_in']
    cuts = [int(c) for c in np.cumsum(IN_SPLITS)[:-1]]
    return jnp.split(z, cuts, axis=-1)


def token_shift(z, prev, mu):
    z_prev = jnp.concatenate([prev[:, None].astype(z.dtype), z[:, :-1]], axis=1)
    return z + mu * (z_prev - z), z[:, -1]


def rwkv7_time_mix(zr, state0, p):
    B, S, _ = zr.shape
    f32 = jnp.float32
    cuts = [RWKV_DIM, 2 * RWKV_DIM, 3 * RWKV_DIM, 3 * RWKV_DIM + DECAY_LORA,
            3 * RWKV_DIM + DECAY_LORA + ICLR_LORA]
    r, k, v, wd, ad, gd = jnp.split(zr, cuts, axis=-1)
    w_log = -jax.nn.softplus(-(p['rw_w0'] + jnp.tanh(wd) @ p['rw_w2']).astype(f32)) - 0.5
    decay = jnp.exp(-jnp.exp(w_log))
    a = jax.nn.sigmoid((p['rw_a0'] + ad @ p['rw_a2']).astype(f32))
    g = jax.nn.sigmoid(gd) @ p['rw_g2']
    heads = lambda t: t.reshape(B, S, RWKV_HEADS, HEAD_DIM).astype(f32)
    kk = heads(k * p['rw_k_k'])
    kk = kk / jnp.maximum(jnp.sqrt(jnp.sum(kk * kk, axis=-1, keepdims=True)), 1e-12)
    k = k.astype(f32) * (1.0 + (a - 1.0) * p['rw_k_a'])
    rh, kh, vh, ah, wh = heads(r), heads(k), heads(v), heads(a), heads(decay)

    def step(st, inp):
        r_t, k_t, v_t, kk_t, a_t, w_t = inp
        sa = jnp.einsum('bhvk,bhk->bhv', st, -kk_t)
        st = (st * w_t[:, :, None, :] + sa[..., None] * (kk_t * a_t)[:, :, None, :]
              + v_t[..., None] * k_t[:, :, None, :])
        return st, jnp.einsum('bhvk,bhk->bhv', st, r_t)

    xs = tuple(jnp.swapaxes(t, 0, 1) for t in (rh, kh, vh, kk, ah, wh))
    st, y = lax.scan(step, state0.astype(f32), xs)
    y = jnp.swapaxes(y, 0, 1)
    mu = jnp.mean(y, axis=-1, keepdims=True)
    var = jnp.mean(jnp.square(y - mu), axis=-1, keepdims=True)
    yn = ((y - mu) * lax.rsqrt(var + GN_EPS)).reshape(B, S, RWKV_DIM) * p['rw_ln_w'] + p['rw_ln_b']
    bonus = jnp.sum(rh * kh * p['rw_r_k'], axis=-1, keepdims=True) * vh
    o = (yn + bonus.reshape(B, S, RWKV_DIM)) * g
    return o.astype(zr.dtype), st.astype(state0.dtype)


def nsa_project(z_q, z_cmp, z_sel, z_win, z_g, p):
    B, S = z_q.shape[:2]
    kn = p['nsa_k_norm']
    q = rms_norm(z_q.reshape(B, S, NSA_HEADS, HEAD_DIM), p['nsa_q_norm'])
    rows = lambda z: z.reshape(B, S, 2, NSA_GROUPS, HEAD_DIM)
    kv_cmp = rows(z_cmp)
    kv_sel = rows(z_sel)
    kv_sel = jnp.stack([rms_norm(kv_sel[:, :, 0], kn[1]), kv_sel[:, :, 1]], axis=2)
    kv_win = rows(z_win)
    kv_win = jnp.stack([rms_norm(kv_win[:, :, 0], kn[2]), kv_win[:, :, 1]], axis=2)
    gates = jax.nn.sigmoid(z_g).reshape(B, S, NSA_BRANCHES, NSA_HEADS)
    return q, kv_cmp, kv_sel, kv_win, gates


def nsa_compress(kv, p):
    B, T = kv.shape[:2]
    r = CMP_LEN // CMP_STRIDE
    n_chunks = T // CMP_STRIDE
    n_cmp = n_chunks - r + 1
    c = kv[:, :n_chunks * CMP_STRIDE].reshape(B, n_chunks, CMP_STRIDE, 2, NSA_GROUPS, HEAD_DIM)
    c = jnp.transpose(c, (0, 1, 3, 4, 2, 5)).reshape(B, n_chunks, 2, NSA_GROUPS, CMP_STRIDE * HEAD_DIM)
    w1 = p['cmp_w1'].reshape(2, r, CMP_STRIDE * HEAD_DIM, CMP_HIDDEN)
    part = jnp.einsum('bnkgf,krfh->bnkgrh', c, w1)
    pe = jnp.transpose(p['cmp_pe'], (1, 0, 2)).reshape(2, CMP_LEN * HEAD_DIM)
    pe_term = jnp.einsum('kf,kfh->kh', pe, p['cmp_w1'])[:, None, :]
    pre = sum(part[:, i:i + n_cmp, :, :, i] for i in range(r)) + pe_term
    return jnp.einsum('bnkgh,khd->bnkgd', jax.nn.gelu(pre), p['cmp_w2'])


def nsa_cmp_branch(q, kv_rows, qpos, p, rel_bias):
    kvc = nsa_compress(kv_rows, p)
    kc = rms_norm(kvc[:, :, 0], p['nsa_k_norm'][0])
    vc = kvc[:, :, 1]
    B, Q = q.shape[:2]
    n_cmp = kc.shape[1]
    qg = q.reshape(B, Q, NSA_GROUPS, NSA_HPG, HEAD_DIM)
    logits = jnp.einsum('bqghd,bngd->bghqn', qg, kc).astype(jnp.float32) * NSA_SCALE
    end = jnp.arange(n_cmp) * CMP_STRIDE + CMP_LEN - 1
    dist = qpos[:, None] - end[None, :]
    pr = masked_softmax(logits + head_bias(rel_bias, dist), dist >= 0)
    o = jnp.einsum('bghqn,bngd->bqghd', pr.astype(vc.dtype), vc)
    return o.reshape(B, Q, NSA_HEADS, HEAD_DIM), pr


def nsa_select(p_cmp, qpos, n_slc, n_top):
    n_cmp = p_cmp.shape[-1]
    imp = jnp.sum(p_cmp, axis=2)
    cstart = jnp.arange(n_cmp) * CMP_STRIDE
    sstart = jnp.arange(n_slc) * SEL_BLOCK
    overlap = ((cstart[:, None] < sstart[None, :] + SEL_BLOCK)
               & (cstart[:, None] + CMP_LEN > sstart[None, :])).astype(jnp.float32)
    score = jnp.einsum('bgqn,nj->bgqj', imp, overlap)
    cur = qpos // SEL_BLOCK
    j = jnp.arange(n_slc)[None, :]
    valid = j <= cur[:, None]
    forced = (j == 0) | (j == cur[:, None]) | (j == cur[:, None] - 1)
    score = jnp.where(forced, FORCED_SCORE, jnp.where(valid, score, -jnp.inf))
    return lax.top_k(score, n_top)[1]


def nsa_sel_attend(q, rows, kpos, qpos, rel_bias):
    B, G, Q, n, L = rows.shape[:5]
    kv = rows.reshape(B, G, Q, n * L, 2, HEAD_DIM)
    k, v = kv[..., 0, :], kv[..., 1, :]
    kpos = kpos.reshape(B, G, Q, n * L)
    qg = q.reshape(B, Q, NSA_GROUPS, NSA_HPG, HEAD_DIM)
    logits = jnp.einsum('bqghd,bgqkd->bghqk', qg, k).astype(jnp.float32) * NSA_SCALE
    dist = qpos[None, None, :, None] - kpos
    rb = rel_bias.reshape(NUM_BUCKETS, NSA_GROUPS, NSA_HPG)
    bias = rb[t5_bucket(dist), jnp.arange(NSA_GROUPS)[None, :, None, None]]
    logits = logits + jnp.moveaxis(bias, -1, 2).astype(jnp.float32)
    pr = masked_softmax(logits, (dist >= 0)[:, :, None])
    o = jnp.einsum('bghqk,bgqkd->bqghd', pr.astype(v.dtype), v)
    return o.reshape(B, Q, NSA_HEADS, HEAD_DIM)


def nsa_win_attend(q, rows, kpos, qpos, rel_bias):
    B, Q = q.shape[:2]
    k, v = rows[:, :, 0], rows[:, :, 1]
    qg = q.reshape(B, Q, NSA_GROUPS, NSA_HPG, HEAD_DIM)
    logits = jnp.einsum('bqghd,bkgd->bghqk', qg, k).astype(jnp.float32) * NSA_SCALE
    dist = qpos[:, None] - kpos[None, :]
    mask = (dist >= 0) & (dist < WINDOW) & (kpos >= 0)[None, :]
    pr = masked_softmax(logits + head_bias(rel_bias, dist), mask)
    o = jnp.einsum('bghqk,bkgd->bqghd', pr.astype(v.dtype), v)
    return o.reshape(B, Q, NSA_HEADS, HEAD_DIM)


def nsa_combine(gates, o_cmp, o_sel, o_win):
    B, S = gates.shape[:2]
    o = jnp.einsum('bsch,bschd->bshd', gates, jnp.stack([o_cmp, o_sel, o_win], axis=2).astype(gates.dtype))
    return o.reshape(B, S, NSA_DIM)


def nsa_prompt(q, kv_cmp, kv_sel, kv_win, gates, p, rel_bias):
    B, S = q.shape[:2]
    pos = jnp.arange(S)
    o_cmp, p_cmp = nsa_cmp_branch(q, kv_cmp, pos, p, rel_bias)
    n_slc = S // SEL_BLOCK
    idx = nsa_select(p_cmp, pos, n_slc, min(TOP_N, n_slc))
    sel_blocks = kv_sel.reshape(B, n_slc, SEL_BLOCK, 2, NSA_GROUPS, HEAD_DIM)
    win_pad = jnp.pad(kv_win, ((0, 0), (WINDOW, 0), (0, 0), (0, 0), (0, 0)))
    bidx = jnp.arange(B)[:, None, None, None]
    gidx = jnp.arange(NSA_GROUPS)[None, :, None, None]

    def block(i):
        q0 = i * Q_BLOCK
        qb = lax.dynamic_slice_in_dim(q, q0, Q_BLOCK, axis=1)
        qpos = q0 + jnp.arange(Q_BLOCK)
        ib = lax.dynamic_slice_in_dim(idx, q0, Q_BLOCK, axis=2)
        rows = sel_blocks[bidx, ib, :, :, gidx]
        kpos = ib[..., None] * SEL_BLOCK + jnp.arange(SEL_BLOCK)
        o_sel = nsa_sel_attend(qb, rows, kpos, qpos, rel_bias)
        wrows = lax.dynamic_slice_in_dim(win_pad, q0, WINDOW + Q_BLOCK, axis=1)
        wpos = q0 - WINDOW + jnp.arange(WINDOW + Q_BLOCK)
        o_win = nsa_win_attend(qb, wrows, wpos, qpos, rel_bias)
        return o_sel, o_win

    o_sel, o_win = lax.map(block, jnp.arange(S // Q_BLOCK))
    o_sel = jnp.moveaxis(o_sel, 0, 1).reshape(B, S, NSA_HEADS, HEAD_DIM)
    o_win = jnp.moveaxis(o_win, 0, 1).reshape(B, S, NSA_HEADS, HEAD_DIM)
    return nsa_combine(gates, o_cmp, o_sel, o_win)


def nsa_sample(q, kv_cmp, kv_sel, kv_win, gates, p, rel_bias, pool_cmp, pool_sel, win_buf, page_table):
    B, Q = q.shape[:2]
    P = page_table.shape[1] * PAGE_SIZE
    qpos = P + jnp.arange(Q)
    rows_cmp = jnp.concatenate([gather_pages(pool_cmp, page_table), kv_cmp], axis=1)
    o_cmp, p_cmp = nsa_cmp_branch(q, rows_cmp, qpos, p, rel_bias)
    nbp = P // SEL_BLOCK
    nbn = -(-Q // SEL_BLOCK)
    n_slc = nbp + nbn
    idx = nsa_select(p_cmp, qpos, n_slc, min(TOP_N, n_slc))
    bpp = PAGE_SIZE // SEL_BLOCK
    pool_b = pool_sel.reshape((pool_sel.shape[0], bpp, SEL_BLOCK) + pool_sel.shape[2:])
    jp = jnp.minimum(idx, nbp - 1)
    phys = jnp.take_along_axis(page_table, (jp // bpp).reshape(B, -1), axis=1).reshape(idx.shape)
    bidx = jnp.arange(B)[:, None, None, None]
    gidx = jnp.arange(NSA_GROUPS)[None, :, None, None]
    past_rows = pool_b[phys, jp % bpp, :, :, gidx]
    new_b = jnp.pad(kv_sel, ((0, 0), (0, nbn * SEL_BLOCK - Q), (0, 0), (0, 0), (0, 0)))
    new_b = new_b.reshape(B, nbn, SEL_BLOCK, 2, NSA_GROUPS, HEAD_DIM)
    new_rows = new_b[bidx, jnp.clip(idx - nbp, 0, nbn - 1), :, :, gidx]
    rows = jnp.where((idx >= nbp)[..., None, None, None], new_rows, past_rows)
    kpos = idx[..., None] * SEL_BLOCK + jnp.arange(SEL_BLOCK)
    o_sel = nsa_sel_attend(q, rows, kpos, qpos, rel_bias)
    wb = win_buf.shape[1]
    wrows = jnp.concatenate([win_buf, kv_win], axis=1)
    wpos = P - wb + jnp.arange(wb + Q)
    o_win = nsa_win_attend(q, wrows, wpos, qpos, rel_bias)
    new_win = wrows[:, wb + Q - min(WINDOW, P + Q):]
    return nsa_combine(gates, o_cmp, o_sel, o_win), new_win


def mla_project(dq, dkv, pos, p):
    B, S = dq.shape[:2]
    cq = rms_norm(dq, p['mla_q_a_norm'])
    q = (cq @ p['mla_w_uq']).reshape(B, S, MLA_HEADS, NOPE_DIM + ROPE_DIM)
    q_nope = rms_norm(q[..., :NOPE_DIM], p['mla_q_nope_norm'])
    q_rope = rope(rms_norm(q[..., NOPE_DIM:], p['mla_q_rope_norm']), pos)
    c_kv = rms_norm(dkv[..., :KV_LORA], p['mla_kv_a_norm'])
    k_rope = rope(rms_norm(dkv[..., KV_LORA:], p['mla_k_rope_norm'])[:, :, None, :], pos)[:, :, 0]
    return q_nope, q_rope, jnp.concatenate([c_kv, k_rope], axis=-1)


def mla_keys(lat, p):
    c = lat[..., :KV_LORA]
    kr = lat[..., KV_LORA:]
    k_nope = rms_norm(jnp.einsum('bkc,chd->bkhd', c, p['mla_w_uk']), p['mla_k_nope_norm'])
    return c, kr, k_nope


def mla_attend(q_nope, q_rope, c, kr, k_nope, kpos, qpos, p):
    B, Q = q_nope.shape[:2]
    logits = (jnp.einsum('bqhd,bkhd->bhqk', q_nope, k_nope)
              + jnp.einsum('bqhd,bkd->bhqk', q_rope, kr)).astype(jnp.float32) * MLA_SCALE
    pr = masked_softmax(logits, kpos[None, :] <= qpos[:, None])
    o_lat = jnp.einsum('bhqk,bkc->bqhc', pr.astype(c.dtype), c)
    o = jnp.einsum('bqhc,chd->bqhd', o_lat, p['mla_w_uv'])
    return o.reshape(B, Q, MLA_DIM)


def mla_prompt(q_nope, q_rope, lat, p):
    B, S = lat.shape[:2]
    pos = jnp.arange(S)
    c, kr, k_nope = mla_keys(lat, p)

    def block(i):
        q0 = i * Q_BLOCK
        qn = lax.dynamic_slice_in_dim(q_nope, q0, Q_BLOCK, axis=1)
        qr = lax.dynamic_slice_in_dim(q_rope, q0, Q_BLOCK, axis=1)
        return mla_attend(qn, qr, c, kr, k_nope, pos, q0 + jnp.arange(Q_BLOCK), p)

    o = lax.map(block, jnp.arange(S // Q_BLOCK))
    return jnp.moveaxis(o, 0, 1).reshape(B, S, MLA_DIM)


def merge_branches(o_a, o_b, o_c, z_merge, p):
    B, S = o_a.shape[:2]
    g = jax.nn.sigmoid(z_merge).reshape(B, S, N_BRANCH, D_MODEL)
    h = (g[:, :, 0] * (o_a @ p['w_br_a']) + g[:, :, 1] * (o_b @ p['w_br_b'])
         + g[:, :, 2] * (o_c @ p['w_br_c']))
    return h @ p['w_out']


def hier_moe(x, p):
    B, S, D = x.shape
    t = x.reshape(B * S, D)
    gl = (t @ p['w_router_group'] + p['b_router_group']).astype(jnp.float32)
    _, gi = lax.top_k(gl, 1)
    gw = jnp.take_along_axis(jax.nn.softmax(gl, axis=-1), gi, axis=-1)
    el = (t @ p['w_router_expert'] + p['b_router_expert']).astype(jnp.float32)
    el = jnp.take_along_axis(el.reshape(-1, N_GROUPS, EXPERTS_PER_GROUP), gi[:, :, None], axis=1)[:, 0]
    ev, ei = lax.top_k(el, TOP_K_IN_GROUP)
    ew = jax.nn.softmax(ev, axis=-1) * gw
    eid = gi * EXPERTS_PER_GROUP + ei
    combine = jnp.sum(jax.nn.one_hot(eid, N_EXPERTS, dtype=jnp.float32) * ew[..., None], axis=1)
    hg = jnp.einsum('td,edf->tef', t, p['w_gate'])
    hu = jnp.einsum('td,edf->tef', t, p['w_up'])
    act = jax.nn.silu(hg) * hu * combine[:, :, None].astype(t.dtype)
    return jnp.einsum('tef,efd->td', act, p['w_down']).reshape(B, S, D)


def prompt_layer(x, p, rel_bias):
    B, S, _ = x.shape
    pos = jnp.arange(S)
    z_rw, z_q, z_cmp, z_sel, z_win, z_g, z_dq, z_dkv, z_merge = mixer_inputs(x, p)
    zr, shift_last = token_shift(z_rw, jnp.zeros((B, RWKV_COLS), x.dtype), p['rw_mu'])
    o_a, st = rwkv7_time_mix(zr, jnp.zeros((B, RWKV_HEADS, HEAD_DIM, HEAD_DIM), x.dtype), p)
    q, kv_cmp, kv_sel, kv_win, gates = nsa_project(z_q, z_cmp, z_sel, z_win, z_g, p)
    o_b = nsa_prompt(q, kv_cmp, kv_sel, kv_win, gates, p, rel_bias)
    q_nope, q_rope, lat = mla_project(z_dq, z_dkv, pos, p)
    o_c = mla_prompt(q_nope, q_rope, lat, p)
    x = x + merge_branches(o_a, o_b, o_c, z_merge, p)
    x = x + hier_moe(rms_norm(x, p['norm_ffn']), p)
    return x, (lat, kv_cmp, kv_sel, kv_win[:, S - min(WINDOW, S):], st, shift_last)


def sample_layer(x, p, rel_bias, pool_mla, pool_cmp, pool_sel, win_buf, st_rwkv, st_shift, page_table):
    B, Q, _ = x.shape
    P = page_table.shape[1] * PAGE_SIZE
    qpos = P + jnp.arange(Q)
    z_rw, z_q, z_cmp, z_sel, z_win, z_g, z_dq, z_dkv, z_merge = mixer_inputs(x, p)
    zr, shift_last = token_shift(z_rw, st_shift, p['rw_mu'])
    o_a, st = rwkv7_time_mix(zr, st_rwkv, p)
    q, kv_cmp, kv_sel, kv_win, gates = nsa_project(z_q, z_cmp, z_sel, z_win, z_g, p)
    o_b, new_win = nsa_sample(q, kv_cmp, kv_sel, kv_win, gates, p, rel_bias, pool_cmp, pool_sel,
                              win_buf, page_table)
    q_nope, q_rope, lat = mla_project(z_dq, z_dkv, qpos, p)
    lat_all = jnp.concatenate([gather_pages(pool_mla, page_table), lat], axis=1)
    c, kr, k_nope = mla_keys(lat_all, p)
    o_c = mla_attend(q_nope, q_rope, c, kr, k_nope, jnp.arange(P + Q), qpos, p)
    x = x + merge_branches(o_a, o_b, o_c, z_merge, p)
    x = x + hier_moe(rms_norm(x, p['norm_ffn']), p)
    return x, (lat, kv_cmp, kv_sel, new_win, st, shift_last)


def setup_inputs(seed: int = 0) -> dict:
    key = jax.random.key(seed)
    ks = iter(jax.random.split(key, 64))
    nrm = lambda shape, scale: jax.random.normal(next(ks), shape, jnp.float32) * scale
    gain = lambda shape: 1.0 + nrm(shape, 0.02)
    n_pages = PAST_LEN // PAGE_SIZE
    n_used = DEC_BATCH * n_pages
    n_pool = n_used + n_used // POOL_EXTRA_DIV
    win_buf = min(WINDOW, PAST_LEN)
    kv_row = (2, NSA_GROUPS, HEAD_DIM)
    return {
        'x_prompt': nrm((BATCH, SEQ, D_MODEL), 1.0),
        'x_sample': nrm((DEC_BATCH, DEC_SEQ, D_MODEL), 1.0),
        'cache_mla': nrm((DEPTH, n_pool, PAGE_SIZE, KV_LORA + ROPE_DIM), 1.0),
        'cache_nsa_cmp': nrm((DEPTH, n_pool, PAGE_SIZE) + kv_row, 1.0),
        'cache_nsa_sel': nrm((DEPTH, n_pool, PAGE_SIZE) + kv_row, 1.0),
        'state_nsa_win': nrm((DEPTH, DEC_BATCH, win_buf) + kv_row, 1.0),
        'state_rwkv': nrm((DEPTH, DEC_BATCH, RWKV_HEADS, HEAD_DIM, HEAD_DIM), 0.5),
        'state_rwkv_shift': nrm((DEPTH, DEC_BATCH, RWKV_COLS), 1.0),
        'page_table': jax.random.permutation(next(ks), n_pool)[:n_used].reshape(DEC_BATCH, n_pages).astype(jnp.int32),
        'rel_bias': nrm((NUM_BUCKETS, NSA_HEADS), 0.5),
        'norm_mix': gain((DEPTH, D_MODEL)),
        'w_in': nrm((DEPTH, D_MODEL, IN_COLS), D_MODEL ** -0.5),
        'rw_mu': jax.random.uniform(next(ks), (DEPTH, RWKV_COLS), jnp.float32),
        'rw_w0': nrm((DEPTH, RWKV_DIM), 0.3),
        'rw_w2': nrm((DEPTH, DECAY_LORA, RWKV_DIM), 0.5 * DECAY_LORA ** -0.5),
        'rw_a0': nrm((DEPTH, RWKV_DIM), 0.3),
        'rw_a2': nrm((DEPTH, ICLR_LORA, RWKV_DIM), ICLR_LORA ** -0.5),
        'rw_g2': nrm((DEPTH, GATE_LORA, RWKV_DIM), GATE_LORA ** -0.5),
        'rw_k_k': 0.85 + nrm((DEPTH, RWKV_DIM), 0.02),
        'rw_k_a': gain((DEPTH, RWKV_DIM)),
        'rw_r_k': nrm((DEPTH, RWKV_HEADS, HEAD_DIM), 0.1),
        'rw_ln_w': gain((DEPTH, RWKV_DIM)),
        'rw_ln_b': nrm((DEPTH, RWKV_DIM), 0.02),
        'nsa_q_norm': gain((DEPTH, HEAD_DIM)),
        'nsa_k_norm': gain((DEPTH, NSA_BRANCHES, HEAD_DIM)),
        'cmp_pe': nrm((DEPTH, CMP_LEN, 2, HEAD_DIM), 0.1),
        'cmp_w1': nrm((DEPTH, 2, CMP_LEN * HEAD_DIM, CMP_HIDDEN), (CMP_LEN * HEAD_DIM) ** -0.5),
        'cmp_w2': nrm((DEPTH, 2, CMP_HIDDEN, HEAD_DIM), CMP_HIDDEN ** -0.5),
        'mla_q_a_norm': gain((DEPTH, Q_LORA)),
        'mla_w_uq': nrm((DEPTH, Q_LORA, MLA_HEADS * (NOPE_DIM + ROPE_DIM)), Q_LORA ** -0.5),
        'mla_kv_a_norm': gain((DEPTH, KV_LORA)),
        'mla_w_uk': nrm((DEPTH, KV_LORA, MLA_HEADS, NOPE_DIM), KV_LORA ** -0.5),
        'mla_w_uv': nrm((DEPTH, KV_LORA, MLA_HEADS, V_DIM), KV_LORA ** -0.5),
        'mla_q_nope_norm': gain((DEPTH, NOPE_DIM)),
        'mla_q_rope_norm': gain((DEPTH, ROPE_DIM)),
        'mla_k_nope_norm': gain((DEPTH, NOPE_DIM)),
        'mla_k_rope_norm': gain((DEPTH, ROPE_DIM)),
        'w_br_a': nrm((DEPTH, RWKV_DIM, D_MODEL), RWKV_DIM ** -0.5),
        'w_br_b': nrm((DEPTH, NSA_DIM, D_MODEL), NSA_DIM ** -0.5),
        'w_br_c': nrm((DEPTH, MLA_DIM, D_MODEL), MLA_DIM ** -0.5),
        'w_out': nrm((DEPTH, D_MODEL, D_MODEL), D_MODEL ** -0.5),
        'norm_ffn': gain((DEPTH, D_MODEL)),
        'w_router_group': nrm((DEPTH, D_MODEL, N_GROUPS), D_MODEL ** -0.5),
        'b_router_group': nrm((DEPTH, N_GROUPS), 0.01),
        'w_router_expert': nrm((DEPTH, D_MODEL, N_EXPERTS), D_MODEL ** -0.5),
        'b_router_expert': nrm((DEPTH, N_EXPERTS), 0.01),
        'w_gate': nrm((DEPTH, N_EXPERTS, D_MODEL, EXPERT_FF), D_MODEL ** -0.5),
        'w_up': nrm((DEPTH, N_EXPERTS, D_MODEL, EXPERT_FF), D_MODEL ** -0.5),
        'w_down': nrm((DEPTH, N_EXPERTS, EXPERT_FF, D_MODEL), EXPERT_FF ** -0.5),
    }


def reference(x_prompt, x_sample, cache_mla, cache_nsa_cmp, cache_nsa_sel, state_nsa_win, state_rwkv,
              state_rwkv_shift, page_table, rel_bias, norm_mix, w_in, rw_mu, rw_w0, rw_w2, rw_a0, rw_a2,
              rw_g2, rw_k_k, rw_k_a, rw_r_k, rw_ln_w, rw_ln_b, nsa_q_norm, nsa_k_norm, cmp_pe, cmp_w1, cmp_w2,
              mla_q_a_norm, mla_w_uq, mla_kv_a_norm, mla_w_uk, mla_w_uv, mla_q_nope_norm, mla_q_rope_norm,
              mla_k_nope_norm, mla_k_rope_norm, w_br_a, w_br_b, w_br_c, w_out, norm_ffn, w_router_group,
              b_router_group, w_router_expert, b_router_expert, w_gate, w_up, w_down):
    xp, xs = x_prompt, x_sample
    new_p, new_s = [], []
    for l in range(DEPTH):
        p = dict(norm_mix=norm_mix[l], w_in=w_in[l], rw_mu=rw_mu[l], rw_w0=rw_w0[l], rw_w2=rw_w2[l],
                 rw_a0=rw_a0[l], rw_a2=rw_a2[l], rw_g2=rw_g2[l], rw_k_k=rw_k_k[l], rw_k_a=rw_k_a[l],
                 rw_r_k=rw_r_k[l], rw_ln_w=rw_ln_w[l], rw_ln_b=rw_ln_b[l], nsa_q_norm=nsa_q_norm[l],
                 nsa_k_norm=nsa_k_norm[l], cmp_pe=cmp_pe[l], cmp_w1=cmp_w1[l], cmp_w2=cmp_w2[l],
                 mla_q_a_norm=mla_q_a_norm[l], mla_w_uq=mla_w_uq[l], mla_kv_a_norm=mla_kv_a_norm[l],
                 mla_w_uk=mla_w_uk[l], mla_w_uv=mla_w_uv[l], mla_q_nope_norm=mla_q_nope_norm[l],
                 mla_q_rope_norm=mla_q_rope_norm[l], mla_k_nope_norm=mla_k_nope_norm[l],
                 mla_k_rope_norm=mla_k_rope_norm[l], w_br_a=w_br_a[l], w_br_b=w_br_b[l], w_br_c=w_br_c[l],
                 w_out=w_out[l], norm_ffn=norm_ffn[l], w_router_group=w_router_group[l],
                 b_router_group=b_router_group[l], w_router_expert=w_router_expert[l],
                 b_router_expert=b_router_expert[l], w_gate=w_gate[l], w_up=w_up[l], w_down=w_down[l])
        xp, st_p = prompt_layer(xp, p, rel_bias)
        xs, st_s = sample_layer(xs, p, rel_bias, cache_mla[l], cache_nsa_cmp[l], cache_nsa_sel[l],
                                state_nsa_win[l], state_rwkv[l], state_rwkv_shift[l], page_table)
        new_p.append(st_p)
        new_s.append(st_s)
    stk = lambda states, i: jnp.stack([s[i] for s in states])
    return (xp, xs, stk(new_p, 0), stk(new_s, 0), stk(new_p, 1), stk(new_s, 1), stk(new_p, 2), stk(new_s, 2),
            stk(new_p, 3), stk(new_s, 3), stk(new_p, 4), stk(new_s, 4), stk(new_p, 5), stk(new_s, 5))
```

```python
import functools
import math

import jax
import jax.numpy as jnp
import numpy as np
from jax import lax
from jax.experimental import pallas as pl
from jax.experimental.pallas import tpu as pltpu

F32 = jnp.float32
BF16 = jnp.bfloat16

HEAD_DIM = 64
NORM_EPS = 1e-6
PAGE_SIZE = 128
SUBLANES = 8

RWKV_HEADS = 4
RWKV_DIM = RWKV_HEADS * HEAD_DIM
DECAY_LORA = 64
ICLR_LORA = 64
GATE_LORA = 128
RWKV_COLS = 3 * RWKV_DIM + DECAY_LORA + ICLR_LORA + GATE_LORA
GN_EPS = 64e-5

NSA_HEADS = 8
NSA_GROUPS = 2
NSA_HPG = NSA_HEADS // NSA_GROUPS
NSA_DIM = NSA_HEADS * HEAD_DIM
NSA_KV_COLS = 2 * NSA_GROUPS * HEAD_DIM
NSA_BRANCHES = 3
CMP_LEN = 32
CMP_STRIDE = 16
CMP_HIDDEN = 128
SEL_BLOCK = 64
TOP_N = 16
WINDOW = 512
FORCED_SCORE = 1e9
NSA_SCALE = HEAD_DIM ** -0.5

NUM_BUCKETS = 32
MAX_EXACT = NUM_BUCKETS // 2
REL_MAX_DIST = 128

MLA_HEADS = 4
Q_LORA = 192
KV_LORA = 128
NOPE_DIM = 64
ROPE_DIM = 32
V_DIM = 64
MLA_DIM = MLA_HEADS * V_DIM
MLA_QK_PAD = 128
ROPE_THETA = 10000.0
MLA_SCALE = (NOPE_DIM + ROPE_DIM) ** -0.5

N_BRANCH = 3
N_GROUPS = 4
EXPERTS_PER_GROUP = 8
N_EXPERTS = N_GROUPS * EXPERTS_PER_GROUP
TOP_K_IN_GROUP = 2
EXPERT_FF = 256

IN_SPLITS = (RWKV_COLS, NSA_DIM, NSA_KV_COLS, NSA_KV_COLS, NSA_KV_COLS,
             NSA_BRANCHES * NSA_HEADS, Q_LORA, KV_LORA + ROPE_DIM, N_BRANCH * 1024)

VMEM_LIMIT_BYTES = 56 * 1024 * 1024
NEG_BIG = -1e30


def _cparams(sem):
    return pltpu.CompilerParams(dimension_semantics=sem, vmem_limit_bytes=VMEM_LIMIT_BYTES)


def _const_spec(shape):
    nd = len(shape)
    return pl.BlockSpec(shape, lambda *a: (0,) * nd, pipeline_mode=pl.Buffered(1))


def _group_matrix(ids_a, ids_b=None):
    ids_a = np.asarray(ids_a)
    ids_b = ids_a if ids_b is None else np.asarray(ids_b)
    m = (ids_a[:, None] == ids_b[None, :]) & (ids_a[:, None] >= 0)
    return jnp.asarray(m.astype(np.float32), dtype=BF16)


def _gsum(x, pmat):
    return jnp.dot(x.astype(BF16), pmat, preferred_element_type=F32)


def _in_proj_kernel(x_ref, nm_ref, w_rw_ref, w_q_ref, w_kv_ref, w_g_ref, w_dq_ref, w_lat_ref, w_latsw_ref,
                    w_kr_ref, w_krsw_ref, w_mg_ref, w_uq_ref, w_uqsw_ref, w_uk_ref,
                    p64_ref, pq_ref, pk_ref,
                    qn_gain_ref, kvk_gain_ref, qa_gain_ref, latc_gain_ref, latr_gain_ref, latrsw_gain_ref,
                    qm_gain_ref, qmsw_gain_ref, qm_inv_ref, km_gain_ref, kmsw_gain_ref, km_inv_ref,
                    cs_lat_ref, sn_lat_ref, cs_h_ref, sn_h_ref,
                    zrw_ref, qn_ref, kvc_ref, kvs_ref, kvw_ref, gates_ref, lat_ref, qm_ref, km_ref, gm_ref):
    x = x_ref[...]
    inv = lax.rsqrt(jnp.mean(x * x, axis=-1, keepdims=True) + NORM_EPS)
    xn = (x * inv * nm_ref[...]).astype(BF16)

    def proj(w_ref):
        return jnp.dot(xn, w_ref[...], preferred_element_type=F32)

    zrw_ref[...] = proj(w_rw_ref)

    zq = proj(w_q_ref)
    msq = _gsum(zq * zq, p64_ref[...]) * (1.0 / HEAD_DIM)
    qn_ref[...] = (zq * lax.rsqrt(msq + NORM_EPS) * qn_gain_ref[...]).astype(BF16)

    zkv = proj(w_kv_ref)
    kvc_ref[...] = zkv[:, 0:NSA_KV_COLS]
    half = NSA_GROUPS * HEAD_DIM
    for i, o_ref in ((1, kvs_ref), (2, kvw_ref)):
        zk = zkv[:, i * NSA_KV_COLS:i * NSA_KV_COLS + half]
        ms = _gsum(zk * zk, p64_ref[0:half, 0:half]) * (1.0 / HEAD_DIM)
        o_ref[:, 0:half] = zk * lax.rsqrt(ms + NORM_EPS) * kvk_gain_ref[i - 1:i, :]
        o_ref[:, half:2 * half] = zkv[:, i * NSA_KV_COLS + half:(i + 1) * NSA_KV_COLS]

    gates_ref[...] = jax.nn.sigmoid(proj(w_g_ref))

    zl = proj(w_lat_ref)
    zc = zl[:, 0:KV_LORA]
    c_inv = lax.rsqrt(jnp.mean(zc * zc, axis=-1, keepdims=True) + NORM_EPS)
    c_kv = zc * c_inv * latc_gain_ref[...]
    zr = zl[:, KV_LORA:2 * KV_LORA]
    zr_sw = proj(w_latsw_ref)
    r_inv = lax.rsqrt(jnp.sum(zr * zr, axis=-1, keepdims=True) * (1.0 / ROPE_DIM) + NORM_EPS)
    k_rope = (zr * latr_gain_ref[...] * cs_lat_ref[...] + zr_sw * latrsw_gain_ref[...] * sn_lat_ref[...]) * r_inv
    lat_ref[:, 0:KV_LORA] = c_kv
    lat_ref[:, KV_LORA:KV_LORA + ROPE_DIM] = k_rope[:, 0:ROPE_DIM]

    cs_h = jnp.concatenate([cs_h_ref[...]] * MLA_HEADS, axis=1)
    sn_h = jnp.concatenate([sn_h_ref[...]] * MLA_HEADS, axis=1)

    zdq = proj(w_dq_ref)
    q_inv = lax.rsqrt(jnp.sum(zdq * zdq, axis=-1, keepdims=True) * (1.0 / Q_LORA) + NORM_EPS)
    cq = (zdq * q_inv * qa_gain_ref[...]).astype(BF16)
    qr = jnp.dot(cq, w_uq_ref[...], preferred_element_type=F32)
    qr_sw = jnp.dot(cq, w_uqsw_ref[...], preferred_element_type=F32)
    q_s = lax.rsqrt(_gsum(qr * qr, pq_ref[...]) * qm_inv_ref[...] + NORM_EPS)
    qm_ref[...] = ((qr * qm_gain_ref[...] * cs_h + qr_sw * qmsw_gain_ref[...] * sn_h) * q_s).astype(BF16)

    kr = jnp.dot(c_kv.astype(BF16), w_uk_ref[...], preferred_element_type=F32) + proj(w_kr_ref)
    kr_sw = proj(w_krsw_ref)
    k_s = lax.rsqrt(_gsum(kr * kr, pk_ref[...]) * km_inv_ref[...] + NORM_EPS)
    km_ref[...] = ((kr * km_gain_ref[...] * cs_h + kr_sw * kmsw_gain_ref[...] * sn_h) * k_s).astype(BF16)

    d_model = gm_ref.shape[1] // N_BRANCH
    for j in range(N_BRANCH):
        zm = jnp.dot(xn, w_mg_ref[:, j * d_model:(j + 1) * d_model], preferred_element_type=F32)
        gm_ref[:, j * d_model:(j + 1) * d_model] = jax.nn.sigmoid(zm).astype(BF16)


def _rope_tables(pos):
    half = ROPE_DIM // 2
    freq = ROPE_THETA ** (-jnp.arange(half, dtype=F32) / half)
    ang = pos.astype(F32)[:, None] * freq[None, :]
    cos, sin = jnp.cos(ang), jnp.sin(ang)
    n = pos.shape[0]
    z = lambda w: jnp.zeros((n, w), F32)
    cs_lat = jnp.concatenate([cos, cos, z(KV_LORA - ROPE_DIM)], axis=1)
    sn_lat = jnp.concatenate([sin, sin, z(KV_LORA - ROPE_DIM)], axis=1)
    pad = MLA_QK_PAD - NOPE_DIM - ROPE_DIM
    cs_h = jnp.concatenate([jnp.ones((n, NOPE_DIM), F32), cos, cos, z(pad)], axis=1)
    sn_h = jnp.concatenate([z(NOPE_DIM), sin, sin, z(pad)], axis=1)
    return cs_lat, sn_lat, cs_h, sn_h


def _swap_halves(w, sign=True):
    h = w.shape[-1] // 2
    a, b = w[..., :h], w[..., h:]
    return jnp.concatenate([-b if sign else b, a], axis=-1)


def _prep_in_weights(p):
    d_model = p['w_in'].shape[0]
    cuts = np.cumsum(IN_SPLITS)
    w = p['w_in']
    w_rw = w[:, :cuts[0]]
    w_q = w[:, cuts[0]:cuts[1]]
    w_kv = w[:, cuts[1]:cuts[4]]
    w_g = jnp.pad(w[:, cuts[4]:cuts[5]], ((0, 0), (0, 128 - NSA_BRANCHES * NSA_HEADS)))
    w_dq = jnp.pad(w[:, cuts[5]:cuts[6]], ((0, 0), (0, 256 - Q_LORA)))
    w_dkv = w[:, cuts[6]:cuts[7]]
    w_mg = w[:, cuts[7]:]
    w_c, w_r = w_dkv[:, :KV_LORA], w_dkv[:, KV_LORA:]
    padr = lambda a, n: jnp.pad(a, ((0, 0), (0, n - a.shape[1])))
    w_lat = jnp.concatenate([w_c, padr(w_r, KV_LORA)], axis=1)
    w_latsw = padr(_swap_halves(w_r), KV_LORA)
    zpad = jnp.zeros((d_model, MLA_QK_PAD - NOPE_DIM - ROPE_DIM), F32)
    znope = jnp.zeros((d_model, NOPE_DIM), F32)
    w_kr = jnp.concatenate([znope, w_r, zpad] * MLA_HEADS, axis=1)
    w_krsw = jnp.concatenate([znope, _swap_halves(w_r), zpad] * MLA_HEADS, axis=1)

    wuq = jnp.pad(p['mla_w_uq'], ((0, 256 - Q_LORA), (0, 0))).reshape(256, MLA_HEADS, NOPE_DIM + ROPE_DIM)
    zq = jnp.zeros((256, MLA_QK_PAD - NOPE_DIM - ROPE_DIM), F32)
    w_uq = jnp.concatenate([jnp.concatenate([wuq[:, h], zq], axis=1) for h in range(MLA_HEADS)], axis=1)
    w_uqsw = jnp.concatenate([jnp.concatenate([jnp.zeros((256, NOPE_DIM), F32), _swap_halves(wuq[:, h, NOPE_DIM:]), zq],
                                              axis=1) for h in range(MLA_HEADS)], axis=1)
    zk = jnp.zeros((KV_LORA, MLA_QK_PAD - NOPE_DIM), F32)
    w_uk = jnp.concatenate([jnp.concatenate([p['mla_w_uk'][:, h], zk], axis=1) for h in range(MLA_HEADS)], axis=1)

    p64 = _group_matrix(np.arange(NSA_DIM) // HEAD_DIM)
    slot = np.arange(MLA_HEADS * MLA_QK_PAD) % MLA_QK_PAD
    head = np.arange(MLA_HEADS * MLA_QK_PAD) // MLA_QK_PAD
    part = np.where(slot < NOPE_DIM, 0, np.where(slot < NOPE_DIM + ROPE_DIM, 1, -1))
    gid = np.where(part >= 0, head * 2 + part, -1)
    pqk = _group_matrix(gid)
    inv_cnt = np.where(part == 0, 1.0 / NOPE_DIM, np.where(part == 1, 1.0 / ROPE_DIM, 1.0)).astype(np.float32)[None]

    row = lambda v: v.reshape(1, -1).astype(F32)
    one_slot = lambda nope, rope: jnp.concatenate([nope, rope, jnp.zeros((MLA_QK_PAD - NOPE_DIM - ROPE_DIM,), F32)])
    qg, qr_g = p['mla_q_nope_norm'], p['mla_q_rope_norm']
    kg, kr_g = p['mla_k_nope_norm'], p['mla_k_rope_norm']
    zn = jnp.zeros((NOPE_DIM,), F32)
    qm_gain = row(jnp.tile(one_slot(qg, qr_g), MLA_HEADS)) * MLA_SCALE
    qmsw_gain = row(jnp.tile(one_slot(zn, _swap_halves(qr_g, sign=False)), MLA_HEADS)) * MLA_SCALE
    km_gain = row(jnp.tile(one_slot(kg, kr_g), MLA_HEADS))
    kmsw_gain = row(jnp.tile(one_slot(zn, _swap_halves(kr_g, sign=False)), MLA_HEADS))
    padv = lambda v, n: jnp.pad(v, (0, n - v.shape[0]))

    b = lambda a: a.astype(BF16)
    return dict(
        nm=row(p['norm_mix']), w_rw=b(w_rw), w_q=b(w_q), w_kv=b(w_kv), w_g=b(w_g), w_dq=b(w_dq), w_lat=b(w_lat),
        w_latsw=b(w_latsw), w_kr=b(w_kr), w_krsw=b(w_krsw), w_mg=b(w_mg), w_uq=b(w_uq), w_uqsw=b(w_uqsw), w_uk=b(w_uk),
        p64=p64, pq=pqk, pk=pqk,
        qn_gain=row(jnp.tile(p['nsa_q_norm'], NSA_HEADS)) * NSA_SCALE,
        kvk_gain=jnp.stack([jnp.tile(p['nsa_k_norm'][1], NSA_GROUPS), jnp.tile(p['nsa_k_norm'][2], NSA_GROUPS)]).astype(F32),
        qa_gain=row(padv(p['mla_q_a_norm'], 256)), latc_gain=row(p['mla_kv_a_norm']),
        latr_gain=row(padv(kr_g, KV_LORA)), latrsw_gain=row(padv(_swap_halves(kr_g, sign=False), KV_LORA)),
        qm_gain=qm_gain, qmsw_gain=qmsw_gain, qm_inv=jnp.asarray(inv_cnt),
        km_gain=km_gain, kmsw_gain=kmsw_gain, km_inv=jnp.asarray(inv_cnt),
    )


_IN_W_ORDER = ('nm', 'w_rw', 'w_q', 'w_kv', 'w_g', 'w_dq', 'w_lat', 'w_latsw', 'w_kr', 'w_krsw', 'w_mg', 'w_uq',
               'w_uqsw', 'w_uk', 'p64', 'pq', 'pk', 'qn_gain', 'kvk_gain', 'qa_gain', 'latc_gain', 'latr_gain',
               'latrsw_gain', 'qm_gain', 'qmsw_gain', 'qm_inv', 'km_gain', 'kmsw_gain', 'km_inv')


def in_proj(x, pos, wts, tile_m=256):
    t, d_model = x.shape
    tile_m = min(tile_m, t)
    assert t % tile_m == 0
    tabs = _rope_tables(pos)
    consts = [wts[k] for k in _IN_W_ORDER]
    row_spec = lambda w: pl.BlockSpec((tile_m, w), lambda i: (i, 0))
    in_specs = ([row_spec(d_model)] + [_const_spec(c.shape) for c in consts] + [row_spec(tb.shape[1]) for tb in tabs])
    outs = dict(zrw=(RWKV_COLS, F32), qn=(NSA_DIM, BF16), kvc=(NSA_KV_COLS, F32), kvs=(NSA_KV_COLS, F32),
                kvw=(NSA_KV_COLS, F32), gates=(128, F32), lat=(KV_LORA + ROPE_DIM, F32),
                qm=(MLA_HEADS * MLA_QK_PAD, BF16), km=(MLA_HEADS * MLA_QK_PAD, BF16), gm=(N_BRANCH * d_model, BF16))
    res = pl.pallas_call(
        _in_proj_kernel,
        out_shape=[jax.ShapeDtypeStruct((t, w), dt) for w, dt in outs.values()],
        grid=(t // tile_m,),
        in_specs=in_specs,
        out_specs=[row_spec(w) for w, _ in outs.values()],
        compiler_params=_cparams(("parallel",)),
        name="in_proj",
    )(x, *consts, *tabs)
    return dict(zip(outs.keys(), res))


def _gsum_hl(x, pmat):
    hi = x.astype(BF16)
    lo = (x - hi.astype(F32)).astype(BF16)
    return jnp.dot(hi, pmat, preferred_element_type=F32) + jnp.dot(lo, pmat, preferred_element_type=F32)


def _rwkv_kernel(z_ref, prev_ref, st0_ref, mu_ref, w0_ref, a0_ref, wwa_ref, g2_ref, kk_ref, ka_ref, rk_ref,
                 lnw_ref, lnb_ref, p64_ref, eye2_ref,
                 o_ref, st_ref,
                 zbuf, carry, s_state, r_sc, w_sc, kk_sc, b_sc, k_sc, v_sc, y_sc):
    j = pl.program_id(1)
    nj = pl.num_programs(1)
    tl = z_ref.shape[1]
    hw = 2 * HEAD_DIM

    @pl.when(j == 0)
    def _():
        carry[0:1, :] = prev_ref[0]
        for pr in range(RWKV_HEADS // 2):
            s_state[pr, :, 0:HEAD_DIM] = st0_ref[0, 2 * pr]
            s_state[pr, :, HEAD_DIM:hw] = st0_ref[0, 2 * pr + 1]

    z = z_ref[0]
    zbuf[8:8 + tl, :] = z
    zbuf[7:8, :] = carry[0:1, :]
    z_prev = zbuf[7:7 + tl, :]
    carry[0:1, :] = z[tl - 1:tl, :]
    zr = z + mu_ref[...] * (z_prev - z)

    r = zr[:, 0:RWKV_DIM]
    k = zr[:, RWKV_DIM:2 * RWKV_DIM]
    v = zr[:, 2 * RWKV_DIM:3 * RWKV_DIM]
    wa = zr[:, 3 * RWKV_DIM:3 * RWKV_DIM + DECAY_LORA + ICLR_LORA]
    gd = zr[:, 3 * RWKV_DIM + DECAY_LORA + ICLR_LORA:]
    lane_wa = lax.broadcasted_iota(jnp.int32, wa.shape, 1)
    wa = jnp.where(lane_wa < DECAY_LORA, jnp.tanh(wa), wa)
    twa = jnp.dot(wa.astype(BF16), wwa_ref[...], preferred_element_type=F32)
    y_w = w0_ref[...] + twa[:, 0:RWKV_DIM]
    w_log = -(jnp.maximum(-y_w, 0.0) + jnp.log(1.0 + jnp.exp(-jnp.abs(y_w)))) - 0.5
    decay = jnp.exp(-jnp.exp(w_log))
    a = jax.nn.sigmoid(a0_ref[...] + twa[:, RWKV_DIM:2 * RWKV_DIM])
    g = jnp.dot(jax.nn.sigmoid(gd).astype(BF16), g2_ref[...], preferred_element_type=F32)
    p64 = p64_ref[...]
    kk = k * kk_ref[...]
    kk = kk / jnp.maximum(jnp.sqrt(_gsum_hl(kk * kk, p64)), 1e-12)
    k2 = k * (1.0 + (a - 1.0) * ka_ref[...])
    bonus = _gsum_hl(r * k2 * rk_ref[...], p64) * v

    tlp = r_sc.shape[0]
    pad = lambda x: x if tlp == tl else jnp.concatenate([x, jnp.zeros((tlp - tl, x.shape[1]), x.dtype)], axis=0)
    r_sc[...] = pad(r)
    w_sc[...] = pad(decay)
    kk_sc[...] = pad(kk)
    b_sc[...] = pad(kk * a)
    k_sc[...] = pad(k2)
    v_sc[...] = pad(v)

    eye2 = eye2_ref[...]
    lane = lax.broadcasted_iota(jnp.int32, (HEAD_DIM, hw), 1)
    left = lane < HEAD_DIM

    def half_sums(x):
        sl = jnp.sum(jnp.where(left, x, 0.0), axis=-1, keepdims=True)
        sr = jnp.sum(jnp.where(left, 0.0, x), axis=-1, keepdims=True)
        return jnp.where(left, sl, sr)

    steps_per_group = min(SUBLANES, tl)

    def group(t8, states):
        r0 = pl.multiple_of(t8 * SUBLANES, SUBLANES)
        new = []
        for pr in range(RWKV_HEADS // 2):
            s2 = states[pr]
            cols = slice(pr * hw, (pr + 1) * hw)
            rr, ww, kkb, bb, k2b, vv = (ref[pl.ds(r0, SUBLANES), cols] for ref in (r_sc, w_sc, kk_sc, b_sc, k_sc, v_sc))
            ys = []
            for i in range(steps_per_group):
                row = lambda blk: blk[i:i + 1, :]
                sa = half_sums(s2 * row(kkb))
                vcol = half_sums(eye2 * row(vv))
                s2 = s2 * row(ww) - sa * row(bb) + vcol * row(k2b)
                ycol = half_sums(s2 * row(rr))
                ys.append(jnp.sum(eye2 * ycol, axis=0, keepdims=True))
            ys += [jnp.zeros((1, hw), F32)] * (SUBLANES - steps_per_group)
            y_sc[pl.ds(r0, SUBLANES), cols] = jnp.concatenate(ys, axis=0)
            new.append(s2)
        return tuple(new)

    states = lax.fori_loop(0, tlp // SUBLANES, group, tuple(s_state[pr] for pr in range(RWKV_HEADS // 2)))
    for pr in range(RWKV_HEADS // 2):
        s_state[pr] = states[pr]

    y = y_sc[0:tl, :]
    mean = _gsum_hl(y, p64) * (1.0 / HEAD_DIM)
    dy = y - mean
    var = _gsum_hl(dy * dy, p64) * (1.0 / HEAD_DIM)
    yn = dy * lax.rsqrt(var + GN_EPS) * lnw_ref[...] + lnb_ref[...]
    o_ref[0] = ((yn + bonus) * g).astype(o_ref.dtype)

    @pl.when(j == nj - 1)
    def _():
        for pr in range(RWKV_HEADS // 2):
            st_ref[0, 2 * pr] = states[pr][:, 0:HEAD_DIM]
            st_ref[0, 2 * pr + 1] = states[pr][:, HEAD_DIM:hw]


def _prep_rwkv_weights(p):
    row = lambda v: v.reshape(1, -1).astype(F32)
    wwa = jnp.zeros((DECAY_LORA + ICLR_LORA, 2 * RWKV_DIM), F32)
    wwa = wwa.at[:DECAY_LORA, :RWKV_DIM].set(p['rw_w2']).at[DECAY_LORA:, RWKV_DIM:].set(p['rw_a2'])
    eye2 = np.concatenate([np.eye(HEAD_DIM, dtype=np.float32)] * 2, axis=1)
    return dict(mu=row(p['rw_mu']), w0=row(p['rw_w0']), a0=row(p['rw_a0']), wwa=wwa.astype(BF16),
                g2=p['rw_g2'].astype(BF16), kk=row(p['rw_k_k']), ka=row(p['rw_k_a']), rk=row(p['rw_r_k']),
                lnw=row(p['rw_ln_w']), lnb=row(p['rw_ln_b']),
                p64=_group_matrix(np.arange(RWKV_DIM) // HEAD_DIM), eye2=jnp.asarray(eye2))


_RWKV_W_ORDER = ('mu', 'w0', 'a0', 'wwa', 'g2', 'kk', 'ka', 'rk', 'lnw', 'lnb', 'p64', 'eye2')


def rwkv_mix(zrw, prev, state0, wts, tile_l=256):
    bsz, length, cols = zrw.shape
    tile_l = min(tile_l, length)
    assert length % tile_l == 0
    consts = [wts[k] for k in _RWKV_W_ORDER]
    tile_pad = -(-tile_l // SUBLANES) * SUBLANES
    seq_scratch = [pltpu.VMEM((tile_pad, RWKV_DIM), F32) for _ in range(7)]
    return pl.pallas_call(
        _rwkv_kernel,
        out_shape=[jax.ShapeDtypeStruct((bsz, length, RWKV_DIM), BF16), jax.ShapeDtypeStruct(state0.shape, F32)],
        grid=(bsz, length // tile_l),
        in_specs=[pl.BlockSpec((1, tile_l, cols), lambda b, j: (b, j, 0)),
                  pl.BlockSpec((1, 1, cols), lambda b, j: (b, 0, 0)),
                  pl.BlockSpec((1,) + state0.shape[1:], lambda b, j: (b, 0, 0, 0))]
                 + [_const_spec(c.shape) for c in consts],
        out_specs=[pl.BlockSpec((1, tile_l, RWKV_DIM), lambda b, j: (b, j, 0)),
                   pl.BlockSpec((1,) + state0.shape[1:], lambda b, j: (b, 0, 0, 0))],
        scratch_shapes=[pltpu.VMEM((tile_l + 8, cols), F32), pltpu.VMEM((8, cols), F32),
                        pltpu.VMEM((RWKV_HEADS // 2, HEAD_DIM, 2 * HEAD_DIM), F32)] + seq_scratch,
        compiler_params=_cparams(("parallel", "arbitrary")),
        name="rwkv_mix",
    )(zrw, prev.reshape(bsz, 1, cols), state0, *consts)


def _t5_bucket(dist):
    n = jnp.maximum(dist, 0)
    nf = jnp.maximum(n, 1).astype(F32)
    large = MAX_EXACT + (jnp.log(nf / MAX_EXACT) / math.log(REL_MAX_DIST / MAX_EXACT)
                         * (NUM_BUCKETS - MAX_EXACT)).astype(jnp.int32)
    return jnp.where(n < MAX_EXACT, n, jnp.minimum(large, NUM_BUCKETS - 1))


def _bias_table(rel_bias, dist, valid=None):
    b = rel_bias[_t5_bucket(dist)].astype(F32)
    if valid is not None:
        b = jnp.where(valid[..., None], b, NEG_BIG)
    return jnp.moveaxis(b, -1, 0)


def _flash_init(m_sc, l_sc, acc_sc):
    m_sc[...] = jnp.full(m_sc.shape, NEG_BIG, F32)
    l_sc[...] = jnp.zeros(l_sc.shape, F32)
    acc_sc[...] = jnp.zeros(acc_sc.shape, F32)


def _flash_update(s, v, valid, m_sc, l_sc, acc_sc, v_feature_major=False):
    if valid is not None:
        s = jnp.where(valid, s, NEG_BIG)
    m_prev = m_sc[...]
    m_new = jnp.maximum(m_prev, jnp.max(s, axis=-1, keepdims=True))
    p = jnp.exp(s - m_new)
    if valid is not None:
        p = jnp.where(valid, p, 0.0)
    alpha = jnp.exp(m_prev - m_new)
    l_sc[...] = alpha * l_sc[...] + jnp.sum(p, axis=-1, keepdims=True)
    pv = _dot_nt(p.astype(BF16), v) if v_feature_major else jnp.dot(p.astype(BF16), v, preferred_element_type=F32)
    acc_sc[...] = alpha * acc_sc[...] + pv
    m_sc[...] = m_new


def _flash_result(l_sc, acc_sc):
    return acc_sc[...] / jnp.maximum(l_sc[...], 1e-30)


def _dot_nt(a, b):
    return lax.dot_general(a, b, (((1,), (1,)), ((), ())), preferred_element_type=F32)


def _gelu_tanh(x):
    return 0.5 * x * (1.0 + jnp.tanh(math.sqrt(2.0 / math.pi) * (x + 0.044715 * (x * x * x))))


def _compress_parts(rows_ref, kv, row0, n_chunks, wbig_ref):
    acc = None
    for jj in range(CMP_STRIDE):
        xs = rows_ref[kv, pl.ds(row0 + jj, n_chunks, stride=CMP_STRIDE), :].astype(BF16)
        t = jnp.dot(xs, wbig_ref[kv, jj], preferred_element_type=F32)
        acc = t if acc is None else acc + t
    return acc


def _compress_finish(part_lo, part_hi, kv, pe_ref, w2_ref):
    outs = []
    for g in range(NSA_GROUPS):
        c0 = g * 2 * CMP_HIDDEN
        pre = part_lo[:, c0:c0 + CMP_HIDDEN] + part_hi[:, c0 + CMP_HIDDEN:c0 + 2 * CMP_HIDDEN] + pe_ref[kv:kv + 1, :]
        outs.append(jnp.dot(_gelu_tanh(pre).astype(BF16), w2_ref[kv], preferred_element_type=F32))
    return outs


def _prep_cmp_weights(p):
    w1 = p['cmp_w1'].reshape(2, 2, CMP_STRIDE, HEAD_DIM, CMP_HIDDEN)
    per_j = jnp.concatenate([w1[:, 0], w1[:, 1]], axis=-1)
    z = jnp.zeros_like(per_j)
    wbig = jnp.concatenate([jnp.concatenate([per_j, z], axis=-1), jnp.concatenate([z, per_j], axis=-1)], axis=2)
    pe = jnp.transpose(p['cmp_pe'], (1, 0, 2)).reshape(2, CMP_LEN * HEAD_DIM)
    pe_term = jnp.einsum('kf,kfh->kh', pe, p['cmp_w1'], precision=lax.Precision.HIGHEST)
    return dict(wbig=wbig.astype(BF16), pe=pe_term.astype(F32), w2=p['cmp_w2'].astype(BF16),
                kn0=p['nsa_k_norm'][0].reshape(1, HEAD_DIM).astype(F32))


def _overlap_matrix(n_cmp, n_slc, rows, cols):
    cstart = np.arange(rows) * CMP_STRIDE
    sstart = np.arange(cols) * SEL_BLOCK
    m = ((cstart[:, None] < sstart[None, :] + SEL_BLOCK) & (cstart[:, None] + CMP_LEN > sstart[None, :])
         & (np.arange(rows)[:, None] < n_cmp) & (np.arange(cols)[None, :] < n_slc))
    return m.astype(np.float32)


def _block_ranks(score, qpos, n_slc):
    lane = lax.broadcasted_iota(jnp.int32, score.shape, 1)
    cur = lax.shift_right_logical(qpos, int(math.log2(SEL_BLOCK)))
    forced = (lane == 0) | (lane == cur) | (lane == cur - 1)
    s = jnp.where(forced, FORCED_SCORE, jnp.where(lane <= cur, score, -jnp.inf))
    s = jnp.where(lane < n_slc, s, -jnp.inf)
    rank = jnp.zeros(score.shape, jnp.int32)
    for i in range(n_slc):
        si = s[:, i:i + 1]
        rank = rank + jnp.where((si > s) | ((si == s) & (i < lane)), 1, 0)
    return jnp.where(lane < n_slc, rank, n_slc)


def _select_blocks(score, qpos, n_slc, n_top):
    return _block_ranks(score, qpos, n_slc) < n_top


def _nsa_cmp_prompt_kernel(rows_ref, q_ref, bias_ref, wbig_ref, pe_ref, w2_ref, kn0_ref, ovl_ref,
                           o_ref, sel_ref, part_sc, kc_sc, vc_sc, *, n_slc, n_top):
    qt = pl.program_id(1)
    tq = q_ref.shape[1]
    n_chunks = kc_sc.shape[1]

    @pl.when(qt == 0)
    def _():
        part_sc[...] = jnp.zeros(part_sc.shape, F32)
        for kv in range(2):
            part_sc[kv, 0:n_chunks, :] = _compress_parts(rows_ref.at[0], kv, 0, n_chunks, wbig_ref)
        for kv in range(2):
            summ = _compress_finish(part_sc[kv, 0:n_chunks, :], part_sc[kv, 1:n_chunks + 1, :], kv, pe_ref, w2_ref)
            for g in range(NSA_GROUPS):
                if kv == 0:
                    kn = summ[g] * lax.rsqrt(jnp.mean(summ[g] * summ[g], axis=-1, keepdims=True) + NORM_EPS) * kn0_ref[...]
                    kc_sc[g] = kn.astype(BF16)
                else:
                    vc_sc[g] = summ[g].astype(BF16)

    qpos = qt * tq + lax.broadcasted_iota(jnp.int32, (tq, 1), 0)
    for g in range(NSA_GROUPS):
        imp = jnp.zeros((tq, n_chunks), F32)
        for hh in range(NSA_HPG):
            h = g * NSA_HPG + hh
            z = _dot_nt(q_ref[0, :, h * HEAD_DIM:(h + 1) * HEAD_DIM], kc_sc[g]) + bias_ref[h]
            valid = bias_ref[h] > 0.5 * NEG_BIG
            m = jnp.max(z, axis=-1, keepdims=True)
            e = jnp.where(valid, jnp.exp(z - m), 0.0)
            pr = e / jnp.maximum(jnp.sum(e, axis=-1, keepdims=True), 1e-30)
            o_ref[0, :, h * HEAD_DIM:(h + 1) * HEAD_DIM] = jnp.dot(pr.astype(BF16), vc_sc[g], preferred_element_type=F32)
            imp = imp + pr
        score = _gsum_hl(imp, ovl_ref[...])
        sel_ref[0, g] = jnp.where(_select_blocks(score, qpos, n_slc, n_top), 1.0, 0.0).astype(sel_ref.dtype)


def nsa_cmp_prompt(kv_cmp, qn, rel_bias, cw, tile_q=256):
    bsz, seq, _ = kv_cmp.shape
    tile_q = min(tile_q, seq)
    n_chunks = seq // CMP_STRIDE
    n_cmp = n_chunks - CMP_LEN // CMP_STRIDE + 1
    n_slc = seq // SEL_BLOCK
    assert n_chunks % 128 == 0 and n_slc <= 128
    half = NSA_GROUPS * HEAD_DIM
    rows = jnp.transpose(kv_cmp.reshape(bsz, seq, 2, half), (0, 2, 1, 3))
    q_pos = jnp.arange(seq)[:, None]
    n_idx = jnp.arange(n_chunks)[None, :]
    dist = q_pos - (n_idx * CMP_STRIDE + CMP_LEN - 1)
    bias = _bias_table(rel_bias, dist, (dist >= 0) & (n_idx < n_cmp))
    ovl = jnp.asarray(_overlap_matrix(n_cmp, n_slc, n_chunks, 128), dtype=BF16)
    consts = [cw['wbig'], cw['pe'], cw['w2'], cw['kn0'], ovl]
    kern = functools.partial(_nsa_cmp_prompt_kernel, n_slc=n_slc, n_top=min(TOP_N, n_slc))
    return pl.pallas_call(
        kern,
        out_shape=[jax.ShapeDtypeStruct((bsz, seq, NSA_DIM), F32),
                   jax.ShapeDtypeStruct((bsz, NSA_GROUPS, seq, 128), BF16)],
        grid=(bsz, seq // tile_q),
        in_specs=[pl.BlockSpec((1, 2, seq, half), lambda b, i: (b, 0, 0, 0)),
                  pl.BlockSpec((1, tile_q, NSA_DIM), lambda b, i: (b, i, 0)),
                  pl.BlockSpec((NSA_HEADS, tile_q, n_chunks), lambda b, i: (0, i, 0))]
                 + [_const_spec(c.shape) for c in consts],
        out_specs=[pl.BlockSpec((1, tile_q, NSA_DIM), lambda b, i: (b, i, 0)),
                   pl.BlockSpec((1, NSA_GROUPS, tile_q, 128), lambda b, i: (b, 0, i, 0))],
        scratch_shapes=[pltpu.VMEM((2, n_chunks + 8, 2 * NSA_GROUPS * CMP_HIDDEN), F32),
                        pltpu.VMEM((NSA_GROUPS, n_chunks, HEAD_DIM), BF16),
                        pltpu.VMEM((NSA_GROUPS, n_chunks, HEAD_DIM), BF16)],
        compiler_params=_cparams(("parallel", "arbitrary")),
        name="nsa_cmp_prompt",
    )(rows, qn, bias, *consts)


def _nsa_prompt_kernel(q_ref, kvs_ref, kvw_ref, sel_ref, ocmp_ref, gates_ref, btile_ref, bfar_ref, eg_ref,
                       o_ref, m_sc, l_sc, acc_sc):
    qt = pl.program_id(1)
    tq = q_ref.shape[1]
    tk = tq
    half = NSA_GROUPS * HEAD_DIM
    ri = lax.broadcasted_iota(jnp.int32, (tq, tk), 0)
    ci = lax.broadcasted_iota(jnp.int32, (tq, tk), 1)
    causal = ci <= ri
    upper = ci > ri
    stack = lambda x: jnp.concatenate([x] * NSA_HPG, axis=0)
    blk_i = lax.broadcasted_iota(jnp.int32, (128, tk), 0)
    blk_of_col = lax.shift_right_logical(lax.broadcasted_iota(jnp.int32, (128, tk), 1), int(math.log2(SEL_BLOCK)))
    blocks_per_tile = tk // SEL_BLOCK

    gates = gates_ref[0]
    g_hi = gates.astype(BF16)
    g_lo = (gates - g_hi.astype(F32)).astype(BF16)
    gexp = (jnp.dot(g_hi, eg_ref[...], preferred_element_type=F32) + jnp.dot(g_lo, eg_ref[...], preferred_element_type=F32))

    o_sel_heads, o_win_heads = [], []
    for g in range(NSA_GROUPS):
        heads = [g * NSA_HPG + hh for hh in range(NSA_HPG)]
        qs = jnp.concatenate([q_ref[0, :, h * HEAD_DIM:(h + 1) * HEAD_DIM] for h in heads], axis=0)
        bias_d0 = jnp.concatenate([btile_ref[h, 0] for h in heads], axis=0)
        bias_d1 = jnp.concatenate([btile_ref[h, 1] for h in heads], axis=0)
        bias_far = jnp.concatenate([jnp.broadcast_to(bfar_ref[h:h + 1, :], (tq, tk)) for h in heads], axis=0)
        selg = sel_ref[0, g]

        def tile(kv_ref, kt, bias, valid):
            r0 = pl.multiple_of(kt * tk, tk)
            k = kv_ref[0, pl.ds(r0, tk), g * HEAD_DIM:(g + 1) * HEAD_DIM].astype(BF16)
            v = kv_ref[0, pl.ds(r0, tk), half + g * HEAD_DIM:half + (g + 1) * HEAD_DIM].astype(BF16)
            _flash_update(_dot_nt(qs, k) + bias, v, valid, m_sc, l_sc, acc_sc)

        def sel_valid(kt):
            e = jnp.where(blk_i == kt * blocks_per_tile + blk_of_col, 1.0, 0.0).astype(BF16)
            return jnp.dot(selg, e, preferred_element_type=F32) > 0.5

        _flash_init(m_sc, l_sc, acc_sc)

        def sel_far(kt, c):
            tile(kvs_ref, kt, bias_far, stack(sel_valid(kt)))
            return c

        lax.fori_loop(0, jnp.maximum(qt - 1, 0), sel_far, 0)

        @pl.when(qt >= 1)
        def _():
            tile(kvs_ref, qt - 1, bias_d1, stack(sel_valid(qt - 1)))

        tile(kvs_ref, qt, bias_d0, stack(sel_valid(qt) & causal))
        o_sel = _flash_result(l_sc, acc_sc)

        _flash_init(m_sc, l_sc, acc_sc)
        n_back = WINDOW // tk

        @pl.when(qt >= n_back)
        def _():
            tile(kvw_ref, qt - n_back, bias_far, stack(upper))

        def win_far(kt, c):
            tile(kvw_ref, kt, bias_far, None)
            return c

        lax.fori_loop(jnp.maximum(qt - n_back + 1, 0), jnp.maximum(qt - 1, 0), win_far, 0)

        @pl.when(qt >= 1)
        def _():
            tile(kvw_ref, qt - 1, bias_d1, None)

        tile(kvw_ref, qt, bias_d0, stack(causal))
        o_win = _flash_result(l_sc, acc_sc)
        for hh in range(NSA_HPG):
            o_sel_heads.append(o_sel[hh * tq:(hh + 1) * tq, :])
            o_win_heads.append(o_win[hh * tq:(hh + 1) * tq, :])

    o_sel = jnp.concatenate(o_sel_heads, axis=1)
    o_win = jnp.concatenate(o_win_heads, axis=1)
    o = (gexp[:, 0:NSA_DIM] * ocmp_ref[0] + gexp[:, NSA_DIM:2 * NSA_DIM] * o_sel
         + gexp[:, 2 * NSA_DIM:3 * NSA_DIM] * o_win)
    o_ref[0] = o.astype(o_ref.dtype)


def _gate_expand_matrix():
    m = np.zeros((128, NSA_BRANCHES * NSA_DIM), np.float32)
    for c in range(NSA_BRANCHES):
        for h in range(NSA_HEADS):
            m[c * NSA_HEADS + h, c * NSA_DIM + h * HEAD_DIM:c * NSA_DIM + (h + 1) * HEAD_DIM] = 1.0
    return jnp.asarray(m, dtype=BF16)


def nsa_prompt(qn, kv_sel, kv_win, selmask, o_cmp, gates, rel_bias, tile=128):
    bsz, seq, _ = qn.shape
    assert seq % tile == 0 and WINDOW % tile == 0 and tile % SEL_BLOCK == 0
    i = jnp.arange(tile)
    d0 = i[:, None] - i[None, :]
    btile = jnp.stack([_bias_table(rel_bias, d0), _bias_table(rel_bias, d0 + tile)], axis=1)
    bfar = jnp.broadcast_to(rel_bias[NUM_BUCKETS - 1].astype(F32)[:, None], (NSA_HEADS, tile))
    eg = _gate_expand_matrix()
    rows = NSA_HPG * tile
    return pl.pallas_call(
        _nsa_prompt_kernel,
        out_shape=jax.ShapeDtypeStruct((bsz, seq, NSA_DIM), BF16),
        grid=(bsz, seq // tile),
        in_specs=[pl.BlockSpec((1, tile, NSA_DIM), lambda b, i: (b, i, 0)),
                  pl.BlockSpec((1, seq, NSA_KV_COLS), lambda b, i: (b, 0, 0)),
                  pl.BlockSpec((1, seq, NSA_KV_COLS), lambda b, i: (b, 0, 0)),
                  pl.BlockSpec((1, NSA_GROUPS, tile, 128), lambda b, i: (b, 0, i, 0)),
                  pl.BlockSpec((1, tile, NSA_DIM), lambda b, i: (b, i, 0)),
                  pl.BlockSpec((1, tile, 128), lambda b, i: (b, i, 0)),
                  _const_spec(btile.shape), _const_spec(bfar.shape), _const_spec(eg.shape)],
        out_specs=pl.BlockSpec((1, tile, NSA_DIM), lambda b, i: (b, i, 0)),
        scratch_shapes=[pltpu.VMEM((rows, 1), F32), pltpu.VMEM((rows, 1), F32), pltpu.VMEM((rows, HEAD_DIM), F32)],
        compiler_params=_cparams(("parallel", "parallel")),
        name="nsa_prompt",
    )(qn, kv_sel, kv_win, selmask, o_cmp, gates, btile, bfar, eg)


def _mla_prompt_kernel(q_ref, k_ref, lat_ref, wuv_ref, o_ref, m_sc, l_sc, acc_sc):
    qt = pl.program_id(1)
    tq = q_ref.shape[1]
    tk = tq
    ri = lax.broadcasted_iota(jnp.int32, (tq, tk), 0)
    ci = lax.broadcasted_iota(jnp.int32, (tq, tk), 1)
    causal = jnp.concatenate([ci <= ri] * MLA_HEADS, axis=0)
    qh = [q_ref[0, :, h * MLA_QK_PAD:(h + 1) * MLA_QK_PAD] for h in range(MLA_HEADS)]

    def tile(kt, valid):
        r0 = pl.multiple_of(kt * tk, tk)
        s = jnp.concatenate([_dot_nt(qh[h], k_ref[0, pl.ds(r0, tk), h * MLA_QK_PAD:(h + 1) * MLA_QK_PAD])
                             for h in range(MLA_HEADS)], axis=0)
        c = lat_ref[0, pl.ds(r0, tk), 0:KV_LORA].astype(BF16)
        _flash_update(s, c, valid, m_sc, l_sc, acc_sc)

    _flash_init(m_sc, l_sc, acc_sc)

    def body(kt, c):
        tile(kt, None)
        return c

    lax.fori_loop(0, qt, body, 0)
    tile(qt, causal)
    o_lat = _flash_result(l_sc, acc_sc).astype(BF16)
    o_ref[0] = jnp.concatenate(
        [jnp.dot(o_lat[h * tq:(h + 1) * tq, :], wuv_ref[:, h * V_DIM:(h + 1) * V_DIM], preferred_element_type=F32)
         for h in range(MLA_HEADS)], axis=1).astype(o_ref.dtype)


def mla_prompt(qm, km, lat, w_uv, tile=128):
    bsz, seq, width = qm.shape
    assert seq % tile == 0
    wuv = w_uv.reshape(KV_LORA, MLA_DIM).astype(BF16)
    rows = MLA_HEADS * tile
    return pl.pallas_call(
        _mla_prompt_kernel,
        out_shape=jax.ShapeDtypeStruct((bsz, seq, MLA_DIM), BF16),
        grid=(bsz, seq // tile),
        in_specs=[pl.BlockSpec((1, tile, width), lambda b, i: (b, i, 0)),
                  pl.BlockSpec((1, seq, width), lambda b, i: (b, 0, 0)),
                  pl.BlockSpec((1, seq, KV_LORA + ROPE_DIM), lambda b, i: (b, 0, 0)),
                  _const_spec(wuv.shape)],
        out_specs=pl.BlockSpec((1, tile, MLA_DIM), lambda b, i: (b, i, 0)),
        scratch_shapes=[pltpu.VMEM((rows, 1), F32), pltpu.VMEM((rows, 1), F32), pltpu.VMEM((rows, KV_LORA), F32)],
        compiler_params=_cparams(("parallel", "parallel")),
        name="mla_prompt",
    )(qm, km, lat, wuv)


ROUTE_GROUP_LANE0 = N_EXPERTS


def _merge_kernel(x_ref, oa_ref, ob_ref, oc_ref, gm_ref, wa_ref, wb_ref, wc_ref, wo_ref, nf_ref, wrh_ref, wrl_ref,
                  br_ref, x1_ref, xn_ref, route_ref):
    d = x_ref.shape[1]
    dot = lambda a, w_ref: jnp.dot(a, w_ref[...], preferred_element_type=F32)
    h = (gm_ref[:, 0:d].astype(F32) * dot(oa_ref[...], wa_ref)
         + gm_ref[:, d:2 * d].astype(F32) * dot(ob_ref[...], wb_ref)
         + gm_ref[:, 2 * d:3 * d].astype(F32) * dot(oc_ref[...], wc_ref))
    x1 = x_ref[...] + dot(h.astype(BF16), wo_ref)
    x1_ref[...] = x1
    t = x1 * lax.rsqrt(jnp.mean(x1 * x1, axis=-1, keepdims=True) + NORM_EPS) * nf_ref[...]
    xn_ref[...] = t
    t_hi = t.astype(BF16)
    t_lo = (t - t_hi.astype(F32)).astype(BF16)
    logits = dot(t_hi, wrh_ref) + dot(t_lo, wrh_ref) + dot(t_hi, wrl_ref) + br_ref[...]

    lane = lax.broadcasted_iota(jnp.int32, logits.shape, 1)
    big = jnp.int32(1 << 20)
    rmax = lambda a: jnp.max(a, axis=-1, keepdims=True)
    first = lambda hit: jnp.min(jnp.where(hit, lane, big), axis=-1, keepdims=True)
    is_g = (lane >= ROUTE_GROUP_LANE0) & (lane < ROUTE_GROUP_LANE0 + N_GROUPS)
    gl = jnp.where(is_g, logits, -jnp.inf)
    gmax = rmax(gl)
    gi = first(gl == gmax) - ROUTE_GROUP_LANE0
    gw = 1.0 / jnp.sum(jnp.where(is_g, jnp.exp(gl - gmax), 0.0), axis=-1, keepdims=True)
    in_group = (lane < N_EXPERTS) & (lax.shift_right_logical(lane, int(math.log2(EXPERTS_PER_GROUP))) == gi)
    el = jnp.where(in_group, logits, -jnp.inf)
    e1 = rmax(el)
    i1 = first(el == e1)
    el2 = jnp.where(lane == i1, -jnp.inf, el)
    e2 = rmax(el2)
    i2 = first(el2 == e2)
    r = jnp.exp(e2 - e1)
    w1 = gw / (1.0 + r)
    w2 = gw * r / (1.0 + r)
    route_ref[...] = jnp.where(lane == 0, i1.astype(F32), jnp.where(lane == 1, i2.astype(F32),
                               jnp.where(lane == 2, w1, jnp.where(lane == 3, w2, 0.0))))


def _prep_merge_weights(p):
    wr = jnp.concatenate([p['w_router_expert'], p['w_router_group']], axis=1)
    wr = jnp.pad(wr, ((0, 0), (0, 128 - wr.shape[1]))).astype(F32)
    wr_hi = wr.astype(BF16)
    wr_lo = (wr - wr_hi.astype(F32)).astype(BF16)
    br = jnp.pad(jnp.concatenate([p['b_router_expert'], p['b_router_group']]), (0, 128 - N_EXPERTS - N_GROUPS))
    return dict(wa=p['w_br_a'].astype(BF16), wb=p['w_br_b'].astype(BF16), wc=p['w_br_c'].astype(BF16),
                wo=p['w_out'].astype(BF16), nf=p['norm_ffn'].reshape(1, -1).astype(F32), wr_hi=wr_hi, wr_lo=wr_lo,
                br=br.reshape(1, -1).astype(F32))


_MERGE_W_ORDER = ('wa', 'wb', 'wc', 'wo', 'nf', 'wr_hi', 'wr_lo', 'br')


def merge_route(x, o_a, o_b, o_c, gm, wts, tile_m=256):
    t, d = x.shape
    tile_m = min(tile_m, t)
    assert t % tile_m == 0
    consts = [wts[k] for k in _MERGE_W_ORDER]
    row = lambda w: pl.BlockSpec((tile_m, w), lambda i: (i, 0))
    return pl.pallas_call(
        _merge_kernel,
        out_shape=[jax.ShapeDtypeStruct((t, d), F32), jax.ShapeDtypeStruct((t, d), F32),
                   jax.ShapeDtypeStruct((t, 128), F32)],
        grid=(t // tile_m,),
        in_specs=[row(d), row(o_a.shape[1]), row(o_b.shape[1]), row(o_c.shape[1]), row(gm.shape[1])]
                 + [_const_spec(c.shape) for c in consts],
        out_specs=[row(d), row(d), row(128)],
        compiler_params=_cparams(("parallel",)),
        name="merge_route",
    )(x, o_a, o_b, o_c, gm, *consts)


MOE_TILE = 128


def _moe_dispatch(eid):
    t = eid.shape[0]
    n_pairs = 2 * t
    n_tiles = -(-n_pairs // MOE_TILE) + N_EXPERTS
    flat = eid.reshape(-1)
    counts = jnp.zeros((N_EXPERTS,), jnp.int32).at[flat].add(1)
    padded = ((counts + MOE_TILE - 1) // MOE_TILE) * MOE_TILE
    seg_end = jnp.cumsum(padded)
    seg_start = seg_end - padded
    first_of_e = jnp.cumsum(counts) - counts
    order = jnp.argsort(flat, stable=True).astype(jnp.int32)
    sorted_e = flat[order]
    slot_sorted = seg_start[sorted_e] + (jnp.arange(n_pairs, dtype=jnp.int32) - first_of_e[sorted_e])
    tok_of_slot = jnp.zeros((n_tiles * MOE_TILE,), jnp.int32).at[slot_sorted].set(order // 2)
    slot_of_pair = jnp.zeros((n_pairs,), jnp.int32).at[order].set(slot_sorted)
    tile_start = jnp.arange(n_tiles, dtype=jnp.int32) * MOE_TILE
    tile_expert = jnp.minimum(jnp.searchsorted(seg_end, tile_start, side='right'), N_EXPERTS - 1).astype(jnp.int32)
    return tok_of_slot.reshape(n_tiles, 1, MOE_TILE), slot_of_pair.reshape(t, 2), tile_expert, n_tiles


def _row_gather(src_ref, idx_ref, n_rows, dst_ref, sem, start):
    def body(r, c):
        cp = pltpu.make_async_copy(src_ref.at[pl.ds(idx_ref[0, 0, r], 1), :], dst_ref.at[pl.ds(r, 1), :], sem)
        if start:
            cp.start()
        else:
            cp.wait()
        return c

    lax.fori_loop(0, n_rows, body, 0)


def _moe_gemm_kernel(te_ref, tok_ref, tok_next_ref, xn_ref, wg_ref, wu_ref, wd_ref, y_ref, xbuf, sems):
    i = pl.program_id(0)
    n = pl.num_programs(0)
    slot = lax.rem(i, 2)

    @pl.when(i == 0)
    def _():
        _row_gather(xn_ref, tok_ref, MOE_TILE, xbuf.at[0], sems.at[0], True)

    @pl.when(i + 1 < n)
    def _():
        _row_gather(xn_ref, tok_next_ref, MOE_TILE, xbuf.at[1 - slot], sems.at[1 - slot], True)

    _row_gather(xn_ref, tok_ref, MOE_TILE, xbuf.at[slot], sems.at[slot], False)
    xg = xbuf[slot].astype(BF16)
    hg = jnp.dot(xg, wg_ref[0, 0].astype(BF16), preferred_element_type=F32)
    hu = jnp.dot(xg, wu_ref[0, 0].astype(BF16), preferred_element_type=F32)
    act = (hg * jax.nn.sigmoid(hg)) * hu
    y_ref[...] = jnp.dot(act.astype(BF16), wd_ref[0, 0].astype(BF16), preferred_element_type=F32)


def _moe_combine_kernel(pos_ref, pos_next_ref, y_ref, x1_ref, route_ref, o_ref, ybuf, sems):
    i = pl.program_id(0)
    n = pl.num_programs(0)
    slot = lax.rem(i, 2)
    rows = 2 * MOE_TILE

    @pl.when(i == 0)
    def _():
        _row_gather(y_ref, pos_ref, rows, ybuf.at[0], sems.at[0], True)

    @pl.when(i + 1 < n)
    def _():
        _row_gather(y_ref, pos_next_ref, rows, ybuf.at[1 - slot], sems.at[1 - slot], True)

    _row_gather(y_ref, pos_ref, rows, ybuf.at[slot], sems.at[slot], False)
    route = route_ref[...]
    o_ref[...] = (x1_ref[...] + route[:, 2:3] * ybuf[slot, 0:MOE_TILE, :] + route[:, 3:4] * ybuf[slot, MOE_TILE:rows, :])


def moe_ffn(x1, xn, route, w_gate, w_up, w_down, layer):
    t, d = x1.shape
    assert t % MOE_TILE == 0
    eid = route[:, 0:2].astype(jnp.int32)
    tok_of_slot, slot_of_pair, tile_expert, n_tiles = _moe_dispatch(eid)
    ff = w_gate.shape[-1]
    idx_spec = lambda w, off, n: pl.BlockSpec((1, 1, w), lambda i, *_: (jnp.minimum(i + off, n - 1), 0, 0),
                                              memory_space=pltpu.SMEM)
    n_tok_tiles = t // MOE_TILE
    y = pl.pallas_call(
        _moe_gemm_kernel,
        out_shape=jax.ShapeDtypeStruct((n_tiles * MOE_TILE, d), F32),
        grid_spec=pltpu.PrefetchScalarGridSpec(
            num_scalar_prefetch=1,
            grid=(n_tiles,),
            in_specs=[idx_spec(MOE_TILE, 0, n_tiles), idx_spec(MOE_TILE, 1, n_tiles),
                      pl.BlockSpec(memory_space=pl.ANY),
                      pl.BlockSpec((1, 1, d, ff), lambda i, te: (layer, te[i], 0, 0)),
                      pl.BlockSpec((1, 1, d, ff), lambda i, te: (layer, te[i], 0, 0)),
                      pl.BlockSpec((1, 1, ff, d), lambda i, te: (layer, te[i], 0, 0))],
            out_specs=pl.BlockSpec((MOE_TILE, d), lambda i, te: (i, 0)),
            scratch_shapes=[pltpu.VMEM((2, MOE_TILE, d), F32), pltpu.SemaphoreType.DMA((2,))]),
        compiler_params=_cparams(("arbitrary",)),
        name="moe_gemm",
    )(tile_expert, tok_of_slot, tok_of_slot, xn, w_gate, w_up, w_down)

    pos = jnp.transpose(slot_of_pair.reshape(n_tok_tiles, MOE_TILE, 2), (0, 2, 1)).reshape(n_tok_tiles, 1, 2 * MOE_TILE)
    return pl.pallas_call(
        _moe_combine_kernel,
        out_shape=jax.ShapeDtypeStruct((t, d), F32),
        grid=(n_tok_tiles,),
        in_specs=[idx_spec(2 * MOE_TILE, 0, n_tok_tiles), idx_spec(2 * MOE_TILE, 1, n_tok_tiles),
                  pl.BlockSpec(memory_space=pl.ANY),
                  pl.BlockSpec((MOE_TILE, d), lambda i: (i, 0)),
                  pl.BlockSpec((MOE_TILE, 128), lambda i: (i, 0))],
        out_specs=pl.BlockSpec((MOE_TILE, d), lambda i: (i, 0)),
        scratch_shapes=[pltpu.VMEM((2, 2 * MOE_TILE, d), F32), pltpu.SemaphoreType.DMA((2,))],
        compiler_params=_cparams(("arbitrary",)),
        name="moe_combine",
    )(pos, pos, y, x1, route)


PAGES_PER_STEP = 16


def _pool_feature_major(pool):
    nd = pool.ndim
    return jnp.transpose(pool, (0, 1) + tuple(range(3, nd)) + (2,))


def _fetch_pages(pool_ref, layer, pt_ref, first_page, n_pages, dst_ref, sem, start):
    for pg in range(n_pages):
        phys = pt_ref[0, 0, first_page + pg]
        cp = pltpu.make_async_copy(pool_ref.at[layer, phys], dst_ref.at[pg], sem)
        if start:
            cp.start()
        else:
            cp.wait()


def _paged_pipeline(fetch, pt_ref, ptn_ref, bufs, sems, n_steps):
    b, c = pl.program_id(0), pl.program_id(1)
    nb = pl.num_programs(0)
    s = b * n_steps + c
    slot = lax.rem(s, 2)

    @pl.when(s == 0)
    def _():
        fetch(pt_ref, 0, bufs.at[0], sems.at[0], True)

    @pl.when(c + 1 < n_steps)
    def _():
        fetch(pt_ref, (c + 1) * PAGES_PER_STEP, bufs.at[1 - slot], sems.at[1 - slot], True)

    @pl.when((c + 1 == n_steps) & (b + 1 < nb))
    def _():
        fetch(ptn_ref, 0, bufs.at[1 - slot], sems.at[1 - slot], True)

    fetch(pt_ref, c * PAGES_PER_STEP, bufs.at[slot], sems.at[slot], False)
    return slot


def _stack_heads(q_ref, g):
    return jnp.concatenate([q_ref[0, :, (g * NSA_HPG + hh) * HEAD_DIM:(g * NSA_HPG + hh + 1) * HEAD_DIM]
                            for hh in range(NSA_HPG)], axis=0)


def _nsa_cmp_sample_kernel(pt_ref, ptn_ref, pool_ref, q_ref, bias_ref, wbig_ref, pe_ref, w2_ref, kn0_ref, ovl_ref,
                           o_ref, idx_ref, pagebuf, sems, rows_sc, part_sc, lg_sc, vc_sc,
                           *, layer, n_steps, n_slc, n_top, qpos0):
    c = pl.program_id(1)
    nq = q_ref.shape[1]
    nck = PAGES_PER_STEP * PAGE_SIZE // CMP_STRIDE
    fetch = lambda pt, first, dst, sem, start: _fetch_pages(pool_ref, layer, pt, first, PAGES_PER_STEP, dst, sem, start)
    slot = _paged_pipeline(fetch, pt_ref, ptn_ref, pagebuf, sems, n_steps)

    @pl.when(c == 0)
    def _():
        part_sc[:, 0:8, :] = jnp.zeros((2, 8, part_sc.shape[2]), F32)

    for pg in range(PAGES_PER_STEP):
        for kv in range(2):
            rows_sc[kv, pg * PAGE_SIZE:(pg + 1) * PAGE_SIZE, :] = pagebuf[slot, pg, kv].T

    for kv in range(2):
        part_sc[kv, 8:8 + nck, :] = _compress_parts(rows_sc, kv, 0, nck, wbig_ref)
    for kv in range(2):
        summ = _compress_finish(part_sc[kv, 7:7 + nck, :], part_sc[kv, 8:8 + nck, :], kv, pe_ref, w2_ref)
        for g in range(NSA_GROUPS):
            if kv == 0:
                kn = summ[g] * lax.rsqrt(jnp.mean(summ[g] * summ[g], axis=-1, keepdims=True) + NORM_EPS) * kn0_ref[...]
                lg_sc[g, c] = _dot_nt(_stack_heads(q_ref, g), kn.astype(BF16))
            else:
                vc_sc[g, pl.ds(pl.multiple_of(c * nck, nck), nck), :] = summ[g].astype(BF16)
    for kv in range(2):
        part_sc[kv, 7:8, :] = part_sc[kv, 7 + nck:8 + nck, :]

    @pl.when(c == n_steps - 1)
    def _():
        qpos = qpos0 + lax.broadcasted_iota(jnp.int32, (nq, 1), 0)
        lane = lax.broadcasted_iota(jnp.int32, (nq, ovl_ref.shape[1]), 1)
        for g in range(NSA_GROUPS):
            bias = bias_ref[g]
            z = lg_sc[g] + bias
            m = jnp.max(jnp.max(z, axis=2, keepdims=True), axis=0, keepdims=True)
            e = jnp.where(bias > 0.5 * NEG_BIG, jnp.exp(z - m), 0.0)
            den = jnp.sum(jnp.sum(e, axis=2, keepdims=True), axis=0, keepdims=True)
            pr = e / jnp.maximum(den, 1e-30)
            o_g = jnp.zeros((NSA_HPG * nq, HEAD_DIM), F32)
            score = jnp.zeros((nq, ovl_ref.shape[1]), F32)
            for cc in range(n_steps):
                o_g = o_g + jnp.dot(pr[cc].astype(BF16), vc_sc[g, cc * nck:(cc + 1) * nck, :], preferred_element_type=F32)
                imp = sum(pr[cc][hh * nq:(hh + 1) * nq, :] for hh in range(NSA_HPG))
                score = score + _gsum_hl(imp, ovl_ref[cc * nck:(cc + 1) * nck, :])
            rank = _block_ranks(score, qpos, n_slc)
            idx = jnp.zeros((nq, 128), jnp.int32)
            lane128 = lax.broadcasted_iota(jnp.int32, (nq, 128), 1)
            for r in range(n_top):
                pick = jnp.sum(jnp.where(rank == r, lane, 0), axis=-1, keepdims=True)
                idx = jnp.where(lane128 == r, pick, idx)
            idx_ref[0, g] = idx
            for hh in range(NSA_HPG):
                h = g * NSA_HPG + hh
                o_ref[0, :, h * HEAD_DIM:(h + 1) * HEAD_DIM] = o_g[hh * nq:(hh + 1) * nq, :]


def _sample_rows(t_heads_q):
    h, q = t_heads_q.shape[:2]
    return t_heads_q.reshape((NSA_GROUPS, NSA_HPG * q) + t_heads_q.shape[2:])


def nsa_cmp_sample(pool, layer, page_table, qn, rel_bias, cw):
    bsz, n_pages = page_table.shape
    nq = qn.shape[1]
    past = n_pages * PAGE_SIZE
    assert n_pages % PAGES_PER_STEP == 0
    n_steps = n_pages // PAGES_PER_STEP
    nck = PAGES_PER_STEP * PAGE_SIZE // CMP_STRIDE
    n_keys = n_steps * nck
    n_chunks = (past + nq) // CMP_STRIDE
    n_cmp = n_chunks - CMP_LEN // CMP_STRIDE + 1
    assert n_chunks * CMP_STRIDE <= past + CMP_STRIDE - 1 and n_chunks <= n_keys
    nbp = past // SEL_BLOCK
    n_slc = nbp + -(-nq // SEL_BLOCK)
    width = -(-n_slc // 128) * 128
    qpos = past + jnp.arange(nq)
    n_of_key = jnp.arange(n_keys) - 1
    dist = qpos[:, None] - (n_of_key[None, :] * CMP_STRIDE + CMP_LEN - 1)
    bias = _bias_table(rel_bias, dist, (dist >= 0) & (n_of_key[None, :] >= 0) & (n_of_key[None, :] < n_cmp))
    bias = jnp.transpose(_sample_rows(bias).reshape(NSA_GROUPS, NSA_HPG * nq, n_steps, nck), (0, 2, 1, 3))
    ovl = np.zeros((n_keys, width), np.float32)
    ovl[1:] = _overlap_matrix(n_cmp, n_slc, n_keys - 1, width)
    ovl = jnp.asarray(ovl, dtype=BF16)
    consts = [cw['wbig'], cw['pe'], cw['w2'], cw['kn0'], ovl]
    half = NSA_GROUPS * HEAD_DIM
    pool3 = _pool_feature_major(pool).reshape(pool.shape[0], pool.shape[1], 2, half, PAGE_SIZE)
    pt3 = page_table.reshape(bsz, 1, n_pages)
    kern = functools.partial(_nsa_cmp_sample_kernel, layer=layer, n_steps=n_steps, n_slc=n_slc,
                             n_top=min(TOP_N, n_slc), qpos0=past)
    smem_pt = lambda off: pl.BlockSpec((1, 1, n_pages), lambda b, c: (jnp.minimum(b + off, bsz - 1), 0, 0),
                                       memory_space=pltpu.SMEM)
    return pl.pallas_call(
        kern,
        out_shape=[jax.ShapeDtypeStruct((bsz, nq, NSA_DIM), F32),
                   jax.ShapeDtypeStruct((bsz, NSA_GROUPS, nq, 128), jnp.int32)],
        grid=(bsz, n_steps),
        in_specs=[smem_pt(0), smem_pt(1), pl.BlockSpec(memory_space=pl.ANY),
                  pl.BlockSpec((1, nq, NSA_DIM), lambda b, c: (b, 0, 0)), _const_spec(bias.shape)]
                 + [_const_spec(x.shape) for x in consts],
        out_specs=[pl.BlockSpec((1, nq, NSA_DIM), lambda b, c: (b, 0, 0)),
                   pl.BlockSpec((1, NSA_GROUPS, nq, 128), lambda b, c: (b, 0, 0, 0))],
        scratch_shapes=[pltpu.VMEM((2, PAGES_PER_STEP, 2, half, PAGE_SIZE), F32), pltpu.SemaphoreType.DMA((2,)),
                        pltpu.VMEM((2, PAGES_PER_STEP * PAGE_SIZE, half), F32),
                        pltpu.VMEM((2, nck + 8, 2 * NSA_GROUPS * CMP_HIDDEN), F32),
                        pltpu.VMEM((NSA_GROUPS, n_steps, NSA_HPG * nq, nck), F32),
                        pltpu.VMEM((NSA_GROUPS, n_keys, HEAD_DIM), BF16)],
        compiler_params=_cparams(("arbitrary", "arbitrary")),
        name="nsa_cmp_sample",
    )(pt3, pt3, pool3, qn, bias, *consts)


def _fetch_sel_blocks(pool_ref, layer, idx_ref, pt_ref, nbp, n_top, nq, dst_ref, sem, start):
    bpp = PAGE_SIZE // SEL_BLOCK
    for g in range(NSA_GROUPS):
        for t in range(nq):
            def body(r, c):
                jp = jnp.minimum(idx_ref[0, g, t, r], nbp - 1)
                phys = pt_ref[0, 0, jp // bpp]
                for kv in range(2):
                    cp = pltpu.make_async_copy(pool_ref.at[layer, phys, kv, g],
                                               dst_ref.at[(g * nq + t) * n_top + r, kv], sem)
                    if start:
                        cp.start()
                    else:
                        cp.wait()
                return c

            lax.fori_loop(0, n_top, body, 0)


def _nsa_sample_kernel(idx_ref, idxn_ref, pt_ref, ptn_ref, pool_ref, q_ref, kvs_ref, kvw_ref, win_ref, ocmp_ref,
                       gates_ref, near_ref, far_ref, wb1_ref, wb2_ref, eg_ref, eye_ref,
                       o_ref, nwin_ref, gbuf, sems, osel_sc, owin_sc, *, layer, nbp, n_top):
    b = pl.program_id(0)
    nb = pl.num_programs(0)
    nq = q_ref.shape[1]
    half = NSA_GROUPS * HEAD_DIM
    slot = lax.rem(b, 2)
    fetch = lambda idx, pt, dst, sem, start: _fetch_sel_blocks(pool_ref, layer, idx, pt, nbp, n_top, nq, dst, sem, start)

    @pl.when(b == 0)
    def _():
        fetch(idx_ref, pt_ref, gbuf.at[0], sems.at[0], True)

    @pl.when(b + 1 < nb)
    def _():
        fetch(idxn_ref, ptn_ref, gbuf.at[1 - slot], sems.at[1 - slot], True)

    fetch(idx_ref, pt_ref, gbuf.at[slot], sems.at[slot], False)

    n_near = near_ref.shape[0] // (NSA_GROUPS * nq)
    wbuf = win_ref[0, 0]
    n_new_pad = wb2_ref.shape[2]
    for g in range(NSA_GROUPS):
        kcol = slice(g * HEAD_DIM, (g + 1) * HEAD_DIM)
        vcol = slice(half + g * HEAD_DIM, half + (g + 1) * HEAD_DIM)
        pad_rows = lambda x, n: jnp.concatenate([x, jnp.zeros((n - x.shape[0], x.shape[1]), x.dtype)], axis=0)
        eye_d = eye_ref[0:HEAD_DIM, 0:HEAD_DIM]
        knew = _dot_nt(eye_d, pad_rows(kvs_ref[0, :, kcol], PAGE_SIZE).astype(BF16))
        vnew = _dot_nt(eye_d, pad_rows(kvs_ref[0, :, vcol], PAGE_SIZE).astype(BF16))
        bpp = PAGE_SIZE // SEL_BLOCK
        for t in range(nq):
            ks, vs, bs = [], [], []
            for r in range(n_top):
                idx = idx_ref[0, g, t, r]
                blk = (g * nq + t) * n_top + r
                is_new = idx >= nbp
                ks.append(jnp.where(is_new, knew, gbuf[slot, blk, 0]).astype(BF16))
                vs.append(jnp.where(is_new, vnew, gbuf[slot, blk, 1]).astype(BF16))
                rel = jnp.clip(idx - (nbp + 1 - n_near), 0, n_near - 1)
                near = near_ref[(rel * NSA_GROUPS + g) * nq + t]
                far = far_ref[g * bpp + lax.rem(jnp.minimum(idx, nbp - 1), bpp)]
                bs.append(jnp.where(idx >= nbp + 1 - n_near, near, far))
            bias = jnp.concatenate(bs, axis=1)
            qgt = jnp.concatenate([q_ref[0, t:t + 1, (g * NSA_HPG + hh) * HEAD_DIM:(g * NSA_HPG + hh + 1) * HEAD_DIM]
                                   for hh in range(NSA_HPG)], axis=0)
            z = jnp.dot(qgt, jnp.concatenate(ks, axis=1), preferred_element_type=F32) + bias
            m = jnp.max(z, axis=-1, keepdims=True)
            e = jnp.where(bias > 0.5 * NEG_BIG, jnp.exp(z - m), 0.0)
            pr = e / jnp.maximum(jnp.sum(e, axis=-1, keepdims=True), 1e-30)
            o = _dot_nt(pr.astype(BF16), jnp.concatenate(vs, axis=1))
            for hh in range(NSA_HPG):
                h = g * NSA_HPG + hh
                osel_sc[t:t + 1, h * HEAD_DIM:(h + 1) * HEAD_DIM] = o[hh:hh + 1, :]
        qs = _stack_heads(q_ref, g)
        b1, b2 = wb1_ref[g], wb2_ref[g]
        z1 = _dot_nt(qs, wbuf[:, kcol].astype(BF16)) + b1
        z2 = _dot_nt(qs, pad_rows(kvw_ref[0, :, kcol], n_new_pad).astype(BF16)) + b2
        m = jnp.maximum(jnp.max(z1, axis=-1, keepdims=True), jnp.max(z2, axis=-1, keepdims=True))
        e1 = jnp.where(b1 > 0.5 * NEG_BIG, jnp.exp(z1 - m), 0.0)
        e2 = jnp.where(b2 > 0.5 * NEG_BIG, jnp.exp(z2 - m), 0.0)
        den = jnp.sum(e1, axis=-1, keepdims=True) + jnp.sum(e2, axis=-1, keepdims=True)
        ow = (jnp.dot(e1.astype(BF16), wbuf[:, vcol].astype(BF16), preferred_element_type=F32)
              + jnp.dot(e2.astype(BF16), pad_rows(kvw_ref[0, :, vcol], n_new_pad).astype(BF16),
                        preferred_element_type=F32)) / jnp.maximum(den, 1e-30)
        for hh in range(NSA_HPG):
            h = g * NSA_HPG + hh
            owin_sc[:, h * HEAD_DIM:(h + 1) * HEAD_DIM] = ow[hh * nq:(hh + 1) * nq, :]

    wlen = wbuf.shape[0]
    nwin_ref[0, 0:wlen - nq, :] = wbuf[nq:wlen, :]
    nwin_ref[0, wlen - nq:wlen, :] = kvw_ref[0]

    gates = gates_ref[0]
    g_hi = gates.astype(BF16)
    g_lo = (gates - g_hi.astype(F32)).astype(BF16)
    gexp = (jnp.dot(g_hi, eg_ref[...], preferred_element_type=F32) + jnp.dot(g_lo, eg_ref[...], preferred_element_type=F32))
    o = (gexp[:, 0:NSA_DIM] * ocmp_ref[0] + gexp[:, NSA_DIM:2 * NSA_DIM] * osel_sc[...]
         + gexp[:, 2 * NSA_DIM:3 * NSA_DIM] * owin_sc[...])
    o_ref[0] = o.astype(o_ref.dtype)


def nsa_sample(pool_sel, win_state, layer, page_table, sel_idx, qn, kv_sel, kv_win, o_cmp, gates, rel_bias):
    bsz, n_pages = page_table.shape
    nq = qn.shape[1]
    past = n_pages * PAGE_SIZE
    nbp = past // SEL_BLOCK
    n_slc = nbp + -(-nq // SEL_BLOCK)
    n_top = min(TOP_N, n_slc)
    wlen = win_state.shape[2]
    assert wlen == WINDOW and past >= WINDOW and nq <= SEL_BLOCK
    qpos = past + jnp.arange(nq)
    n_near = min(n_slc, REL_MAX_DIST // SEL_BLOCK + 1)
    bpp = PAGE_SIZE // SEL_BLOCK
    lane = np.arange(PAGE_SIZE)
    near = []
    for rel in range(n_near):
        blk = n_slc - n_near + rel
        if blk < nbp:
            kpos = (blk // bpp) * PAGE_SIZE + lane
            ok = (lane // SEL_BLOCK) == (blk % bpp)
        else:
            kpos = past + lane
            ok = lane < nq
        dist = qpos[:, None] - jnp.asarray(kpos)[None, :]
        near.append(_bias_table(rel_bias, dist, (dist >= 0) & jnp.asarray(ok)[None, :]))
    near = jnp.stack(near).reshape(n_near, NSA_GROUPS, NSA_HPG, nq, PAGE_SIZE)
    near = jnp.transpose(near, (0, 1, 3, 2, 4)).reshape(n_near * NSA_GROUPS * nq, NSA_HPG, PAGE_SIZE)
    far_h = rel_bias[NUM_BUCKETS - 1].astype(F32).reshape(NSA_GROUPS, 1, NSA_HPG, 1)
    in_half = jnp.asarray((lane[None, :] // SEL_BLOCK) == np.arange(bpp)[:, None]).reshape(1, bpp, 1, PAGE_SIZE)
    far = jnp.where(in_half, far_h, NEG_BIG).reshape(NSA_GROUPS * bpp, NSA_HPG, PAGE_SIZE)
    j = jnp.arange(wlen)
    d1 = qpos[:, None] - (past - wlen + j)[None, :]
    wb1 = _sample_rows(_bias_table(rel_bias, d1, (d1 >= 0) & (d1 < WINDOW)))
    n_new_pad = 128
    tnew = jnp.arange(n_new_pad)
    d2 = qpos[:, None] - (past + tnew)[None, :]
    wb2 = _sample_rows(_bias_table(rel_bias, d2, (d2 >= 0) & (d2 < WINDOW) & (tnew[None, :] < nq)))
    eg = _gate_expand_matrix()
    eye = jnp.asarray(np.eye(128, dtype=np.float32), dtype=BF16)
    consts = [near, far, wb1, wb2, eg, eye]
    pool3 = _pool_feature_major(pool_sel)
    win4 = win_state.reshape(win_state.shape[0], bsz, wlen, NSA_KV_COLS)
    pt3 = page_table.reshape(bsz, 1, n_pages)
    nxt = lambda b: jnp.minimum(b + 1, bsz - 1)
    kern = functools.partial(_nsa_sample_kernel, layer=layer, nbp=nbp, n_top=n_top)
    tok = lambda w: pl.BlockSpec((1, nq, w), lambda b: (b, 0, 0))
    return pl.pallas_call(
        kern,
        out_shape=[jax.ShapeDtypeStruct((bsz, nq, NSA_DIM), BF16), jax.ShapeDtypeStruct((bsz, wlen, NSA_KV_COLS), F32)],
        grid=(bsz,),
        in_specs=[pl.BlockSpec((1, NSA_GROUPS, nq, 128), lambda b: (b, 0, 0, 0), memory_space=pltpu.SMEM),
                  pl.BlockSpec((1, NSA_GROUPS, nq, 128), lambda b: (nxt(b), 0, 0, 0), memory_space=pltpu.SMEM),
                  pl.BlockSpec((1, 1, n_pages), lambda b: (b, 0, 0), memory_space=pltpu.SMEM),
                  pl.BlockSpec((1, 1, n_pages), lambda b: (nxt(b), 0, 0), memory_space=pltpu.SMEM),
                  pl.BlockSpec(memory_space=pl.ANY),
                  tok(NSA_DIM), tok(NSA_KV_COLS), tok(NSA_KV_COLS),
                  pl.BlockSpec((1, 1, wlen, NSA_KV_COLS), lambda b: (layer, b, 0, 0)),
                  tok(NSA_DIM), tok(128)] + [_const_spec(x.shape) for x in consts],
        out_specs=[tok(NSA_DIM), pl.BlockSpec((1, wlen, NSA_KV_COLS), lambda b: (b, 0, 0))],
        scratch_shapes=[pltpu.VMEM((2, NSA_GROUPS * nq * n_top, 2, HEAD_DIM, PAGE_SIZE), F32),
                        pltpu.SemaphoreType.DMA((2,)),
                        pltpu.VMEM((nq, NSA_DIM), F32), pltpu.VMEM((nq, NSA_DIM), F32)],
        compiler_params=_cparams(("arbitrary",)),
        name="nsa_sample",
    )(sel_idx, sel_idx, pt3, pt3, pool3, qn, kv_sel, kv_win, win4, o_cmp, gates, *consts)


def _mla_sample_kernel(pt_ref, ptn_ref, pool_ref, q_ref, kn_ref, latn_ref, wukt_ref, wuv_ref, kg_ref,
                       o_ref, buf, sems, m_sc, l_sc, acc_sc, uq_sc, *, layer, n_steps):
    c = pl.program_id(1)
    nq = q_ref.shape[1]
    nrow = MLA_HEADS * nq
    tk = PAGES_PER_STEP * PAGE_SIZE
    fetch = lambda pt, first, dst, sem, start: _fetch_pages(pool_ref, layer, pt, first, PAGES_PER_STEP, dst, sem, start)
    slot = _paged_pipeline(fetch, pt_ref, ptn_ref, buf, sems, n_steps)
    slot_of = lambda ref, h, lo, hi: ref[0, :, h * MLA_QK_PAD + lo:h * MLA_QK_PAD + hi]

    @pl.when(c == 0)
    def _():
        _flash_init(m_sc, l_sc, acc_sc)
        us, rs = [], []
        for h in range(MLA_HEADS):
            qh = (slot_of(q_ref, h, 0, NOPE_DIM).astype(F32) * kg_ref[...]).astype(BF16)
            us.append(jnp.dot(qh, wukt_ref[h * NOPE_DIM:(h + 1) * NOPE_DIM, :], preferred_element_type=F32))
            rs.append(slot_of(q_ref, h, NOPE_DIM, NOPE_DIM + ROPE_DIM).astype(F32))
        top = jnp.concatenate([jnp.concatenate(us, axis=0), jnp.zeros((nrow, ROPE_DIM), F32)], axis=1)
        bot = jnp.concatenate([jnp.zeros((nrow, KV_LORA), F32), jnp.concatenate(rs, axis=0)], axis=1)
        uq_sc[...] = jnp.concatenate([top, bot], axis=0).astype(BF16)

    lat_t = jnp.concatenate([buf[slot, pg] for pg in range(PAGES_PER_STEP)], axis=1).astype(BF16)
    c_t = lat_t[0:KV_LORA, :]
    knr = jnp.dot(wukt_ref[...], c_t, preferred_element_type=F32)
    ms = jnp.sum((knr * knr).reshape(MLA_HEADS, NOPE_DIM, tk), axis=1) * (1.0 / NOPE_DIM)
    inv = lax.rsqrt(ms + NORM_EPS)
    inv_rows = jnp.concatenate([jnp.broadcast_to(inv[h:h + 1, :], (nq, tk)) for h in range(MLA_HEADS)], axis=0)
    raw = jnp.dot(uq_sc[...], lat_t, preferred_element_type=F32)
    _flash_update(raw[0:nrow, :] * inv_rows + raw[nrow:2 * nrow, :], c_t, None, m_sc, l_sc, acc_sc, v_feature_major=True)

    @pl.when(c == n_steps - 1)
    def _():
        pad_rows = lambda x, n: jnp.concatenate([x, jnp.zeros((n - x.shape[0], x.shape[1]), x.dtype)], axis=0)
        s_new = jnp.concatenate([_dot_nt(slot_of(q_ref, h, 0, MLA_QK_PAD), pad_rows(slot_of(kn_ref, h, 0, MLA_QK_PAD), PAGE_SIZE))
                                 for h in range(MLA_HEADS)], axis=0)
        q_tok = lax.rem(lax.broadcasted_iota(jnp.int32, (nrow, PAGE_SIZE), 0), nq)
        key = lax.broadcasted_iota(jnp.int32, (nrow, PAGE_SIZE), 1)
        v_new = pad_rows(latn_ref[0, :, 0:KV_LORA], PAGE_SIZE).astype(BF16)
        _flash_update(s_new, v_new, (key <= q_tok) & (key < nq), m_sc, l_sc, acc_sc)
        o_lat = _flash_result(l_sc, acc_sc).astype(BF16)
        for h in range(MLA_HEADS):
            o_ref[0, :, h * V_DIM:(h + 1) * V_DIM] = jnp.dot(
                o_lat[h * nq:(h + 1) * nq, :], wuv_ref[:, h * V_DIM:(h + 1) * V_DIM],
                preferred_element_type=F32).astype(o_ref.dtype)


def mla_sample(pool_mla, layer, page_table, qm, km_new, lat_new, p):
    bsz, n_pages = page_table.shape
    nq = qm.shape[1]
    assert n_pages % PAGES_PER_STEP == 0
    n_steps = n_pages // PAGES_PER_STEP
    nrow = MLA_HEADS * nq
    wukt = p['mla_w_uk'].reshape(KV_LORA, MLA_HEADS * NOPE_DIM).T.astype(BF16)
    wuv = p['mla_w_uv'].reshape(KV_LORA, MLA_DIM).astype(BF16)
    kg = p['mla_k_nope_norm'].reshape(1, NOPE_DIM).astype(F32)
    consts = [wukt, wuv, kg]
    pt3 = page_table.reshape(bsz, 1, n_pages)
    width = KV_LORA + ROPE_DIM
    pool_t = _pool_feature_major(pool_mla)
    smem_pt = lambda off: pl.BlockSpec((1, 1, n_pages), lambda b, c: (jnp.minimum(b + off, bsz - 1), 0, 0),
                                       memory_space=pltpu.SMEM)
    tok = lambda w: pl.BlockSpec((1, nq, w), lambda b, c: (b, 0, 0))
    kern = functools.partial(_mla_sample_kernel, layer=layer, n_steps=n_steps)
    return pl.pallas_call(
        kern,
        out_shape=jax.ShapeDtypeStruct((bsz, nq, MLA_DIM), BF16),
        grid=(bsz, n_steps),
        in_specs=[smem_pt(0), smem_pt(1), pl.BlockSpec(memory_space=pl.ANY),
                  tok(qm.shape[2]), tok(km_new.shape[2]), tok(width)] + [_const_spec(x.shape) for x in consts],
        out_specs=tok(MLA_DIM),
        scratch_shapes=[pltpu.VMEM((2, PAGES_PER_STEP, width, PAGE_SIZE), F32), pltpu.SemaphoreType.DMA((2,)),
                        pltpu.VMEM((nrow, 1), F32), pltpu.VMEM((nrow, 1), F32), pltpu.VMEM((nrow, KV_LORA), F32),
                        pltpu.VMEM((2 * nrow, width), BF16)],
        compiler_params=_cparams(("arbitrary", "arbitrary")),
        name="mla_sample",
    )(pt3, pt3, pool_t, qm, km_new, lat_new, *consts)


_LAYER_PARAMS = ('norm_mix', 'w_in', 'rw_mu', 'rw_w0', 'rw_w2', 'rw_a0', 'rw_a2', 'rw_g2', 'rw_k_k', 'rw_k_a', 'rw_r_k',
                 'rw_ln_w', 'rw_ln_b', 'nsa_q_norm', 'nsa_k_norm', 'cmp_pe', 'cmp_w1', 'cmp_w2', 'mla_q_a_norm',
                 'mla_w_uq', 'mla_kv_a_norm', 'mla_w_uk', 'mla_w_uv', 'mla_q_nope_norm', 'mla_q_rope_norm',
                 'mla_k_nope_norm', 'mla_k_rope_norm', 'w_br_a', 'w_br_b', 'w_br_c', 'w_out', 'norm_ffn',
                 'w_router_group', 'b_router_group', 'w_router_expert', 'b_router_expert')


def kernel(x_prompt, x_sample, cache_mla, cache_nsa_cmp, cache_nsa_sel, state_nsa_win, state_rwkv, state_rwkv_shift,
           page_table, rel_bias, norm_mix, w_in, rw_mu, rw_w0, rw_w2, rw_a0, rw_a2, rw_g2, rw_k_k, rw_k_a, rw_r_k,
           rw_ln_w, rw_ln_b, nsa_q_norm, nsa_k_norm, cmp_pe, cmp_w1, cmp_w2, mla_q_a_norm, mla_w_uq, mla_kv_a_norm,
           mla_w_uk, mla_w_uv, mla_q_nope_norm, mla_q_rope_norm, mla_k_nope_norm, mla_k_rope_norm, w_br_a, w_br_b,
           w_br_c, w_out, norm_ffn, w_router_group, b_router_group, w_router_expert, b_router_expert, w_gate, w_up,
           w_down):
    stacked = dict(zip(_LAYER_PARAMS, (norm_mix, w_in, rw_mu, rw_w0, rw_w2, rw_a0, rw_a2, rw_g2, rw_k_k, rw_k_a, rw_r_k,
                                       rw_ln_w, rw_ln_b, nsa_q_norm, nsa_k_norm, cmp_pe, cmp_w1, cmp_w2, mla_q_a_norm,
                                       mla_w_uq, mla_kv_a_norm, mla_w_uk, mla_w_uv, mla_q_nope_norm, mla_q_rope_norm,
                                       mla_k_nope_norm, mla_k_rope_norm, w_br_a, w_br_b, w_br_c, w_out, norm_ffn,
                                       w_router_group, b_router_group, w_router_expert, b_router_expert)))
    depth = w_in.shape[0]
    bp, seq, d_model = x_prompt.shape
    bs, nq, _ = x_sample.shape
    past = page_table.shape[1] * PAGE_SIZE
    xp = x_prompt.reshape(bp * seq, d_model)
    xs = x_sample.reshape(bs * nq, d_model)
    pos_p = jnp.tile(jnp.arange(seq, dtype=jnp.int32), bp)
    pos_s = jnp.tile(past + jnp.arange(nq, dtype=jnp.int32), bs)
    r3p = lambda a: a.reshape(bp, seq, a.shape[-1])
    r3s = lambda a: a.reshape(bs, nq, a.shape[-1])
    flat = lambda a: a.reshape(-1, a.shape[-1])
    kv_shape = (2, NSA_GROUPS, HEAD_DIM)
    new_p, new_s = [], []
    for l in range(depth):
        p = {k: v[l] for k, v in stacked.items()}
        w_inp, w_rw, w_cmp, w_mg = _prep_in_weights(p), _prep_rwkv_weights(p), _prep_cmp_weights(p), _prep_merge_weights(p)
        ip = in_proj(xp, pos_p, w_inp)
        iq = in_proj(xs, pos_s, w_inp)

        zrw_p, zrw_s = r3p(ip['zrw']), r3s(iq['zrw'])
        oa_p, st_p = rwkv_mix(zrw_p, jnp.zeros((bp, RWKV_COLS), F32), jnp.zeros((bp,) + state_rwkv.shape[2:], F32), w_rw)
        oa_s, st_s = rwkv_mix(zrw_s, state_rwkv_shift[l], state_rwkv[l], w_rw)

        ocmp_p, sel_p = nsa_cmp_prompt(r3p(ip['kvc']), r3p(ip['qn']), rel_bias, w_cmp)
        ob_p = nsa_prompt(r3p(ip['qn']), r3p(ip['kvs']), r3p(ip['kvw']), sel_p, ocmp_p, r3p(ip['gates']), rel_bias)
        oc_p = mla_prompt(r3p(ip['qm']), r3p(ip['km']), r3p(ip['lat']), p['mla_w_uv'])

        ocmp_s, idx_s = nsa_cmp_sample(cache_nsa_cmp, l, page_table, r3s(iq['qn']), rel_bias, w_cmp)
        ob_s, nwin_s = nsa_sample(cache_nsa_sel, state_nsa_win, l, page_table, idx_s, r3s(iq['qn']), r3s(iq['kvs']),
                                  r3s(iq['kvw']), ocmp_s, r3s(iq['gates']), rel_bias)
        oc_s = mla_sample(cache_mla, l, page_table, r3s(iq['qm']), r3s(iq['km']), r3s(iq['lat']), p)

        x1p, xnp, rt_p = merge_route(xp, flat(oa_p), flat(ob_p), flat(oc_p), ip['gm'], w_mg)
        xp = moe_ffn(x1p, xnp, rt_p, w_gate, w_up, w_down, l)
        x1s, xns, rt_s = merge_route(xs, flat(oa_s), flat(ob_s), flat(oc_s), iq['gm'], w_mg)
        xs = moe_ffn(x1s, xns, rt_s, w_gate, w_up, w_down, l)

        wkeep = min(WINDOW, seq)
        new_p.append((r3p(ip['lat']), r3p(ip['kvc']).reshape((bp, seq) + kv_shape),
                      r3p(ip['kvs']).reshape((bp, seq) + kv_shape),
                      r3p(ip['kvw'])[:, seq - wkeep:].reshape((bp, wkeep) + kv_shape), st_p, zrw_p[:, -1]))
        new_s.append((r3s(iq['lat']), r3s(iq['kvc']).reshape((bs, nq) + kv_shape),
                      r3s(iq['kvs']).reshape((bs, nq) + kv_shape),
                      nwin_s.reshape((bs, nwin_s.shape[1]) + kv_shape), st_s, zrw_s[:, -1]))
    stk = lambda states, i: jnp.stack([s[i] for s in states])
    return (xp.reshape(bp, seq, d_model), xs.reshape(bs, nq, d_model),
            stk(new_p, 0), stk(new_s, 0), stk(new_p, 1), stk(new_s, 1), stk(new_p, 2), stk(new_s, 2),
            stk(new_p, 3), stk(new_s, 3), stk(new_p, 4), stk(new_s, 4), stk(new_p, 5), stk(new_s, 5))
```

```python
import functools
import math

import jax
import jax.numpy as jnp
import numpy as np
from jax import lax
from jax.experimental import pallas as pl
from jax.experimental.pallas import tpu as pltpu

F32 = jnp.float32
BF16 = jnp.bfloat16

HEAD_DIM = 64
NORM_EPS = 1e-6
PAGE_SIZE = 128
SUBLANES = 8

RWKV_HEADS = 4
RWKV_DIM = RWKV_HEADS * HEAD_DIM
DECAY_LORA = 64
ICLR_LORA = 64
GATE_LORA = 128
RWKV_COLS = 3 * RWKV_DIM + DECAY_LORA + ICLR_LORA + GATE_LORA
GN_EPS = 64e-5

NSA_HEADS = 8
NSA_GROUPS = 2
NSA_HPG = NSA_HEADS // NSA_GROUPS
NSA_DIM = NSA_HEADS * HEAD_DIM
NSA_KV_COLS = 2 * NSA_GROUPS * HEAD_DIM
NSA_BRANCHES = 3
CMP_LEN = 32
CMP_STRIDE = 16
CMP_HIDDEN = 128
SEL_BLOCK = 64
TOP_N = 16
WINDOW = 512
FORCED_SCORE = 1e9
NSA_SCALE = HEAD_DIM ** -0.5

NUM_BUCKETS = 32
MAX_EXACT = NUM_BUCKETS // 2
REL_MAX_DIST = 128

MLA_HEADS = 4
Q_LORA = 192
KV_LORA = 128
NOPE_DIM = 64
ROPE_DIM = 32
V_DIM = 64
MLA_DIM = MLA_HEADS * V_DIM
MLA_QK_PAD = 128
ROPE_THETA = 10000.0
MLA_SCALE = (NOPE_DIM + ROPE_DIM) ** -0.5

N_BRANCH = 3
N_GROUPS = 4
EXPERTS_PER_GROUP = 8
N_EXPERTS = N_GROUPS * EXPERTS_PER_GROUP
TOP_K_IN_GROUP = 2
EXPERT_FF = 256

IN_SPLITS = (RWKV_COLS, NSA_DIM, NSA_KV_COLS, NSA_KV_COLS, NSA_KV_COLS,
             NSA_BRANCHES * NSA_HEADS, Q_LORA, KV_LORA + ROPE_DIM, N_BRANCH * 1024)

VMEM_LIMIT_BYTES = 56 * 1024 * 1024
NEG_BIG = -1e30


def _cparams(sem):
    return pltpu.CompilerParams(dimension_semantics=sem, vmem_limit_bytes=VMEM_LIMIT_BYTES)


def _const_spec(shape):
    nd = len(shape)
    return pl.BlockSpec(shape, lambda *a: (0,) * nd, pipeline_mode=pl.Buffered(1))


def _group_matrix(ids_a, ids_b=None):
    ids_a = np.asarray(ids_a)
    ids_b = ids_a if ids_b is None else np.asarray(ids_b)
    m = (ids_a[:, None] == ids_b[None, :]) & (ids_a[:, None] >= 0)
    return jnp.asarray(m.astype(np.float32), dtype=BF16)


def _gsum(x, pmat):
    return jnp.dot(x.astype(BF16), pmat, preferred_element_type=F32)


def _in_proj_kernel(x_ref, nm_ref, w_rw_ref, w_q_ref, w_kv_ref, w_g_ref, w_dq_ref, w_lat_ref, w_latsw_ref,
                    w_kr_ref, w_krsw_ref, w_mg_ref, w_uq_ref, w_uqsw_ref, w_uk_ref,
                    p64_ref, pq_ref, pk_ref,
                    qn_gain_ref, kvk_gain_ref, qa_gain_ref, latc_gain_ref, latr_gain_ref, latrsw_gain_ref,
                    qm_gain_ref, qmsw_gain_ref, qm_inv_ref, km_gain_ref, kmsw_gain_ref, km_inv_ref,
                    cs_lat_ref, sn_lat_ref, cs_h_ref, sn_h_ref,
                    zrw_ref, qn_ref, kvc_ref, kvs_ref, kvw_ref, gates_ref, lat_ref, qm_ref, km_ref, gm_ref):
    x = x_ref[...]
    inv = lax.rsqrt(jnp.mean(x * x, axis=-1, keepdims=True) + NORM_EPS)
    xn = (x * inv * nm_ref[...]).astype(BF16)

    def proj(w_ref):
        return jnp.dot(xn, w_ref[...], preferred_element_type=F32)

    zrw_ref[...] = proj(w_rw_ref)

    zq = proj(w_q_ref)
    msq = _gsum(zq * zq, p64_ref[...]) * (1.0 / HEAD_DIM)
    qn_ref[...] = (zq * lax.rsqrt(msq + NORM_EPS) * qn_gain_ref[...]).astype(BF16)

    zkv = proj(w_kv_ref)
    kvc_ref[...] = zkv[:, 0:NSA_KV_COLS]
    half = NSA_GROUPS * HEAD_DIM
    for i, o_ref in ((1, kvs_ref), (2, kvw_ref)):
        zk = zkv[:, i * NSA_KV_COLS:i * NSA_KV_COLS + half]
        ms = _gsum(zk * zk, p64_ref[0:half, 0:half]) * (1.0 / HEAD_DIM)
        o_ref[:, 0:half] = zk * lax.rsqrt(ms + NORM_EPS) * kvk_gain_ref[i - 1:i, :]
        o_ref[:, half:2 * half] = zkv[:, i * NSA_KV_COLS + half:(i + 1) * NSA_KV_COLS]

    gates_ref[...] = jax.nn.sigmoid(proj(w_g_ref))

    zl = proj(w_lat_ref)
    zc = zl[:, 0:KV_LORA]
    c_inv = lax.rsqrt(jnp.mean(zc * zc, axis=-1, keepdims=True) + NORM_EPS)
    c_kv = zc * c_inv * latc_gain_ref[...]
    zr = zl[:, KV_LORA:2 * KV_LORA]
    zr_sw = proj(w_latsw_ref)
    r_inv = lax.rsqrt(jnp.sum(zr * zr, axis=-1, keepdims=True) * (1.0 / ROPE_DIM) + NORM_EPS)
    k_rope = (zr * latr_gain_ref[...] * cs_lat_ref[...] + zr_sw * latrsw_gain_ref[...] * sn_lat_ref[...]) * r_inv
    lat_ref[:, 0:KV_LORA] = c_kv
    lat_ref[:, KV_LORA:KV_LORA + ROPE_DIM] = k_rope[:, 0:ROPE_DIM]

    cs_h = jnp.concatenate([cs_h_ref[...]] * MLA_HEADS, axis=1)
    sn_h = jnp.concatenate([sn_h_ref[...]] * MLA_HEADS, axis=1)

    zdq = proj(w_dq_ref)
    q_inv = lax.rsqrt(jnp.sum(zdq * zdq, axis=-1, keepdims=True) * (1.0 / Q_LORA) + NORM_EPS)
    cq = (zdq * q_inv * qa_gain_ref[...]).astype(BF16)
    qr = jnp.dot(cq, w_uq_ref[...], preferred_element_type=F32)
    qr_sw = jnp.dot(cq, w_uqsw_ref[...], preferred_element_type=F32)
    q_s = lax.rsqrt(_gsum(qr * qr, pq_ref[...]) * qm_inv_ref[...] + NORM_EPS)
    qm_ref[...] = ((qr * qm_gain_ref[...] * cs_h + qr_sw * qmsw_gain_ref[...] * sn_h) * q_s).astype(BF16)

    kr = jnp.dot(c_kv.astype(BF16), w_uk_ref[...], preferred_element_type=F32) + proj(w_kr_ref)
    kr_sw = proj(w_krsw_ref)
    k_s = lax.rsqrt(_gsum(kr * kr, pk_ref[...]) * km_inv_ref[...] + NORM_EPS)
    km_ref[...] = ((kr * km_gain_ref[...] * cs_h + kr_sw * kmsw_gain_ref[...] * sn_h) * k_s).astype(BF16)

    d_model = gm_ref.shape[1] // N_BRANCH
    for j in range(N_BRANCH):
        zm = jnp.dot(xn, w_mg_ref[:, j * d_model:(j + 1) * d_model], preferred_element_type=F32)
        gm_ref[:, j * d_model:(j + 1) * d_model] = jax.nn.sigmoid(zm).astype(BF16)


def _rope_tables(pos):
    half = ROPE_DIM // 2
    freq = ROPE_THETA ** (-jnp.arange(half, dtype=F32) / half)
    ang = pos.astype(F32)[:, None] * freq[None, :]
    cos, sin = jnp.cos(ang), jnp.sin(ang)
    n = pos.shape[0]
    z = lambda w: jnp.zeros((n, w), F32)
    cs_lat = jnp.concatenate([cos, cos, z(KV_LORA - ROPE_DIM)], axis=1)
    sn_lat = jnp.concatenate([sin, sin, z(KV_LORA - ROPE_DIM)], axis=1)
    pad = MLA_QK_PAD - NOPE_DIM - ROPE_DIM
    cs_h = jnp.concatenate([jnp.ones((n, NOPE_DIM), F32), cos, cos, z(pad)], axis=1)
    sn_h = jnp.concatenate([z(NOPE_DIM), sin, sin, z(pad)], axis=1)
    return cs_lat, sn_lat, cs_h, sn_h


def _swap_halves(w, sign=True):
    h = w.shape[-1] // 2
    a, b = w[..., :h], w[..., h:]
    return jnp.concatenate([-b if sign else b, a], axis=-1)


def _prep_in_weights(p):
    d_model = p['w_in'].shape[0]
    cuts = np.cumsum(IN_SPLITS)
    w = p['w_in']
    w_rw = w[:, :cuts[0]]
    w_q = w[:, cuts[0]:cuts[1]]
    w_kv = w[:, cuts[1]:cuts[4]]
    w_g = jnp.pad(w[:, cuts[4]:cuts[5]], ((0, 0), (0, 128 - NSA_BRANCHES * NSA_HEADS)))
    w_dq = jnp.pad(w[:, cuts[5]:cuts[6]], ((0, 0), (0, 256 - Q_LORA)))
    w_dkv = w[:, cuts[6]:cuts[7]]
    w_mg = w[:, cuts[7]:]
    w_c, w_r = w_dkv[:, :KV_LORA], w_dkv[:, KV_LORA:]
    padr = lambda a, n: jnp.pad(a, ((0, 0), (0, n - a.shape[1])))
    w_lat = jnp.concatenate([w_c, padr(w_r, KV_LORA)], axis=1)
    w_latsw = padr(_swap_halves(w_r), KV_LORA)
    zpad = jnp.zeros((d_model, MLA_QK_PAD - NOPE_DIM - ROPE_DIM), F32)
    znope = jnp.zeros((d_model, NOPE_DIM), F32)
    w_kr = jnp.concatenate([znope, w_r, zpad] * MLA_HEADS, axis=1)
    w_krsw = jnp.concatenate([znope, _swap_halves(w_r), zpad] * MLA_HEADS, axis=1)

    wuq = jnp.pad(p['mla_w_uq'], ((0, 256 - Q_LORA), (0, 0))).reshape(256, MLA_HEADS, NOPE_DIM + ROPE_DIM)
    zq = jnp.zeros((256, MLA_QK_PAD - NOPE_DIM - ROPE_DIM), F32)
    w_uq = jnp.concatenate([jnp.concatenate([wuq[:, h], zq], axis=1) for h in range(MLA_HEADS)], axis=1)
    w_uqsw = jnp.concatenate([jnp.concatenate([jnp.zeros((256, NOPE_DIM), F32), _swap_halves(wuq[:, h, NOPE_DIM:]), zq],
                                              axis=1) for h in range(MLA_HEADS)], axis=1)
    zk = jnp.zeros((KV_LORA, MLA_QK_PAD - NOPE_DIM), F32)
    w_uk = jnp.concatenate([jnp.concatenate([p['mla_w_uk'][:, h], zk], axis=1) for h in range(MLA_HEADS)], axis=1)

    p64 = _group_matrix(np.arange(NSA_DIM) // HEAD_DIM)
    slot = np.arange(MLA_HEADS * MLA_QK_PAD) % MLA_QK_PAD
    head = np.arange(MLA_HEADS * MLA_QK_PAD) // MLA_QK_PAD
    part = np.where(slot < NOPE_DIM, 0, np.where(slot < NOPE_DIM + ROPE_DIM, 1, -1))
    gid = np.where(part >= 0, head * 2 + part, -1)
    pqk = _group_matrix(gid)
    inv_cnt = np.where(part == 0, 1.0 / NOPE_DIM, np.where(part == 1, 1.0 / ROPE_DIM, 1.0)).astype(np.float32)[None]

    row = lambda v: v.reshape(1, -1).astype(F32)
    one_slot = lambda nope, rope: jnp.concatenate([nope, rope, jnp.zeros((MLA_QK_PAD - NOPE_DIM - ROPE_DIM,), F32)])
    qg, qr_g = p['mla_q_nope_norm'], p['mla_q_rope_norm']
    kg, kr_g = p['mla_k_nope_norm'], p['mla_k_rope_norm']
    zn = jnp.zeros((NOPE_DIM,), F32)
    qm_gain = row(jnp.tile(one_slot(qg, qr_g), MLA_HEADS)) * MLA_SCALE
    qmsw_gain = row(jnp.tile(one_slot(zn, _swap_halves(qr_g, sign=False)), MLA_HEADS)) * MLA_SCALE
    km_gain = row(jnp.tile(one_slot(kg, kr_g), MLA_HEADS))
    kmsw_gain = row(jnp.tile(one_slot(zn, _swap_halves(kr_g, sign=False)), MLA_HEADS))
    padv = lambda v, n: jnp.pad(v, (0, n - v.shape[0]))

    b = lambda a: a.astype(BF16)
    return dict(
        nm=row(p['norm_mix']), w_rw=b(w_rw), w_q=b(w_q), w_kv=b(w_kv), w_g=b(w_g), w_dq=b(w_dq), w_lat=b(w_lat),
        w_latsw=b(w_latsw), w_kr=b(w_kr), w_krsw=b(w_krsw), w_mg=b(w_mg), w_uq=b(w_uq), w_uqsw=b(w_uqsw), w_uk=b(w_uk),
        p64=p64, pq=pqk, pk=pqk,
        qn_gain=row(jnp.tile(p['nsa_q_norm'], NSA_HEADS)) * NSA_SCALE,
        kvk_gain=jnp.stack([jnp.tile(p['nsa_k_norm'][1], NSA_GROUPS), jnp.tile(p['nsa_k_norm'][2], NSA_GROUPS)]).astype(F32),
        qa_gain=row(padv(p['mla_q_a_norm'], 256)), latc_gain=row(p['mla_kv_a_norm']),
        latr_gain=row(padv(kr_g, KV_LORA)), latrsw_gain=row(padv(_swap_halves(kr_g, sign=False), KV_LORA)),
        qm_gain=qm_gain, qmsw_gain=qmsw_gain, qm_inv=jnp.asarray(inv_cnt),
        km_gain=km_gain, kmsw_gain=kmsw_gain, km_inv=jnp.asarray(inv_cnt),
    )


_IN_W_ORDER = ('nm', 'w_rw', 'w_q', 'w_kv', 'w_g', 'w_dq', 'w_lat', 'w_latsw', 'w_kr', 'w_krsw', 'w_mg', 'w_uq',
               'w_uqsw', 'w_uk', 'p64', 'pq', 'pk', 'qn_gain', 'kvk_gain', 'qa_gain', 'latc_gain', 'latr_gain',
               'latrsw_gain', 'qm_gain', 'qmsw_gain', 'qm_inv', 'km_gain', 'kmsw_gain', 'km_inv')


def in_proj(x, pos, wts, tile_m=256):
    t, d_model = x.shape
    tile_m = min(tile_m, t)
    assert t % tile_m == 0
    tabs = _rope_tables(pos)
    consts = [wts[k] for k in _IN_W_ORDER]
    row_spec = lambda w: pl.BlockSpec((tile_m, w), lambda i: (i, 0))
    in_specs = ([row_spec(d_model)] + [_const_spec(c.shape) for c in consts] + [row_spec(tb.shape[1]) for tb in tabs])
    outs = dict(zrw=(RWKV_COLS, F32), qn=(NSA_DIM, BF16), kvc=(NSA_KV_COLS, F32), kvs=(NSA_KV_COLS, F32),
                kvw=(NSA_KV_COLS, F32), gates=(128, F32), lat=(KV_LORA + ROPE_DIM, F32),
                qm=(MLA_HEADS * MLA_QK_PAD, BF16), km=(MLA_HEADS * MLA_QK_PAD, BF16), gm=(N_BRANCH * d_model, BF16))
    res = pl.pallas_call(
        _in_proj_kernel,
        out_shape=[jax.ShapeDtypeStruct((t, w), dt) for w, dt in outs.values()],
        grid=(t // tile_m,),
        in_specs=in_specs,
        out_specs=[row_spec(w) for w, _ in outs.values()],
        compiler_params=_cparams(("parallel",)),
        name="in_proj",
    )(x, *consts, *tabs)
    return dict(zip(outs.keys(), res))


def _gsum_hl(x, pmat):
    hi = x.astype(BF16)
    lo = (x - hi.astype(F32)).astype(BF16)
    return jnp.dot(hi, pmat, preferred_element_type=F32) + jnp.dot(lo, pmat, preferred_element_type=F32)


def _rwkv_kernel(z_ref, prev_ref, st0_ref, mu_ref, w0_ref, a0_ref, wwa_ref, g2_ref, kk_ref, ka_ref, rk_ref,
                 lnw_ref, lnb_ref, p64_ref, eye2_ref,
                 o_ref, st_ref,
                 zbuf, carry, s_state, r_sc, w_sc, kk_sc, b_sc, k_sc, v_sc, y_sc, bonus_sc, g_sc):
    j = pl.program_id(1)
    nj = pl.num_programs(1)
    nb, tl = z_ref.shape[0], z_ref.shape[1]
    hw = 2 * HEAD_DIM
    n_pairs = RWKV_HEADS // 2

    @pl.when(j == 0)
    def _():
        for s in range(nb):
            carry[s, 0:1, :] = prev_ref[s]
            for pr in range(n_pairs):
                s_state[s, pr, :, 0:HEAD_DIM] = st0_ref[s, 2 * pr]
                s_state[s, pr, :, HEAD_DIM:hw] = st0_ref[s, 2 * pr + 1]

    for s in range(nb):
        _rwkv_prepare(s, z_ref, mu_ref, w0_ref, a0_ref, wwa_ref, g2_ref, kk_ref, ka_ref, rk_ref, p64_ref,
                      zbuf, carry, r_sc, w_sc, kk_sc, b_sc, k_sc, v_sc, bonus_sc, g_sc)

    eye2 = eye2_ref[...]
    ones2 = p64_ref[0:hw, 0:hw]
    lane = lax.broadcasted_iota(jnp.int32, (HEAD_DIM, hw), 1)
    left = lane < HEAD_DIM

    def half_sums(x):
        sl = jnp.sum(jnp.where(left, x, 0.0), axis=-1, keepdims=True)
        sr = jnp.sum(jnp.where(left, 0.0, x), axis=-1, keepdims=True)
        return jnp.where(left, sl, sr)

    tlp = r_sc.shape[1]
    steps_per_group = min(SUBLANES, tl)

    def group(t8, states):
        r0 = pl.multiple_of(t8 * SUBLANES, SUBLANES)
        row = lambda blk, i: blk[i:i + 1, :]
        chains = [(s, pr) for s in range(nb) for pr in range(n_pairs)]
        cols = [slice(pr * hw, (pr + 1) * hw) for _, pr in chains]
        load = lambda ref: [ref[s, pl.ds(r0, SUBLANES), cols[c]] for c, (s, _) in enumerate(chains)]
        rr, ww, kkb, bb, k2b, vv = (load(ref) for ref in (r_sc, w_sc, kk_sc, b_sc, k_sc, v_sc))
        vcols = []
        for c in range(len(chains)):
            vd = jnp.concatenate([eye2 * row(vv[c], i) for i in range(steps_per_group)], axis=0)
            vd_hi = vd.astype(BF16)
            vd_lo = (vd - vd_hi.astype(F32)).astype(BF16)
            vcols.append(jnp.dot(vd_hi, ones2, preferred_element_type=F32)
                         + jnp.dot(vd_lo, ones2, preferred_element_type=F32))
        cur = list(states)
        yps = [[] for _ in chains]
        for i in range(steps_per_group):
            for c in range(len(chains)):
                sa = half_sums(cur[c] * row(kkb[c], i))
                cur[c] = (cur[c] * row(ww[c], i) - sa * row(bb[c], i)
                          + vcols[c][i * HEAD_DIM:(i + 1) * HEAD_DIM, :] * row(k2b[c], i))
                yps[c].append((cur[c] * row(rr[c], i)).astype(BF16))
        for c, (s, _) in enumerate(chains):
            ycols = jnp.dot(jnp.concatenate(yps[c], axis=0), ones2, preferred_element_type=F32)
            ys = [jnp.sum(eye2 * ycols[i * HEAD_DIM:(i + 1) * HEAD_DIM, :], axis=0, keepdims=True)
                  for i in range(steps_per_group)]
            ys += [jnp.zeros((1, hw), F32)] * (SUBLANES - steps_per_group)
            y_sc[s, pl.ds(r0, SUBLANES), cols[c]] = jnp.concatenate(ys, axis=0)
        return tuple(cur)

    init = tuple(s_state[s, pr] for s in range(nb) for pr in range(n_pairs))
    states = lax.fori_loop(0, tlp // SUBLANES, group, init)
    for s in range(nb):
        for pr in range(n_pairs):
            s_state[s, pr] = states[s * n_pairs + pr]

    p64 = p64_ref[...]
    for s in range(nb):
        y = y_sc[s, 0:tl, :]
        mean = _gsum_hl(y, p64) * (1.0 / HEAD_DIM)
        dy = y - mean
        var = _gsum_hl(dy * dy, p64) * (1.0 / HEAD_DIM)
        yn = dy * lax.rsqrt(var + GN_EPS) * lnw_ref[...] + lnb_ref[...]
        o_ref[s] = ((yn + bonus_sc[s, 0:tl, :]) * g_sc[s, 0:tl, :]).astype(o_ref.dtype)

    @pl.when(j == nj - 1)
    def _():
        for s in range(nb):
            for pr in range(n_pairs):
                st_ref[s, 2 * pr] = states[s * n_pairs + pr][:, 0:HEAD_DIM]
                st_ref[s, 2 * pr + 1] = states[s * n_pairs + pr][:, HEAD_DIM:hw]


def _rwkv_prepare(s, z_ref, mu_ref, w0_ref, a0_ref, wwa_ref, g2_ref, kk_ref, ka_ref, rk_ref, p64_ref,
                  zbuf, carry, r_sc, w_sc, kk_sc, b_sc, k_sc, v_sc, bonus_sc, g_sc):
    tl = z_ref.shape[1]
    z = z_ref[s]
    zbuf[8:8 + tl, :] = z
    zbuf[7:8, :] = carry[s, 0:1, :]
    z_prev = zbuf[7:7 + tl, :]
    carry[s, 0:1, :] = z[tl - 1:tl, :]
    zr = z + mu_ref[...] * (z_prev - z)

    r = zr[:, 0:RWKV_DIM]
    k = zr[:, RWKV_DIM:2 * RWKV_DIM]
    v = zr[:, 2 * RWKV_DIM:3 * RWKV_DIM]
    wa = zr[:, 3 * RWKV_DIM:3 * RWKV_DIM + DECAY_LORA + ICLR_LORA]
    gd = zr[:, 3 * RWKV_DIM + DECAY_LORA + ICLR_LORA:]
    lane_wa = lax.broadcasted_iota(jnp.int32, wa.shape, 1)
    wa = jnp.where(lane_wa < DECAY_LORA, jnp.tanh(wa), wa)
    twa = jnp.dot(wa.astype(BF16), wwa_ref[...], preferred_element_type=F32)
    y_w = w0_ref[...] + twa[:, 0:RWKV_DIM]
    w_log = -(jnp.maximum(-y_w, 0.0) + jnp.log(1.0 + jnp.exp(-jnp.abs(y_w)))) - 0.5
    decay = jnp.exp(-jnp.exp(w_log))
    a = jax.nn.sigmoid(a0_ref[...] + twa[:, RWKV_DIM:2 * RWKV_DIM])
    g = jnp.dot(jax.nn.sigmoid(gd).astype(BF16), g2_ref[...], preferred_element_type=F32)
    p64 = p64_ref[...]
    kk = k * kk_ref[...]
    kk = kk / jnp.maximum(jnp.sqrt(_gsum_hl(kk * kk, p64)), 1e-12)
    k2 = k * (1.0 + (a - 1.0) * ka_ref[...])
    bonus = _gsum_hl(r * k2 * rk_ref[...], p64) * v

    tlp = r_sc.shape[1]
    pad = lambda x: x if tlp == tl else jnp.concatenate([x, jnp.zeros((tlp - tl, x.shape[1]), x.dtype)], axis=0)
    r_sc[s] = pad(r)
    w_sc[s] = pad(decay)
    kk_sc[s] = pad(kk)
    b_sc[s] = pad(kk * a)
    k_sc[s] = pad(k2)
    v_sc[s] = pad(v)
    bonus_sc[s] = pad(bonus)
    g_sc[s] = pad(g)


def _prep_rwkv_weights(p):
    row = lambda v: v.reshape(1, -1).astype(F32)
    wwa = jnp.zeros((DECAY_LORA + ICLR_LORA, 2 * RWKV_DIM), F32)
    wwa = wwa.at[:DECAY_LORA, :RWKV_DIM].set(p['rw_w2']).at[DECAY_LORA:, RWKV_DIM:].set(p['rw_a2'])
    eye2 = np.concatenate([np.eye(HEAD_DIM, dtype=np.float32)] * 2, axis=1)
    return dict(mu=row(p['rw_mu']), w0=row(p['rw_w0']), a0=row(p['rw_a0']), wwa=wwa.astype(BF16),
                g2=p['rw_g2'].astype(BF16), kk=row(p['rw_k_k']), ka=row(p['rw_k_a']), rk=row(p['rw_r_k']),
                lnw=row(p['rw_ln_w']), lnb=row(p['rw_ln_b']),
                p64=_group_matrix(np.arange(RWKV_DIM) // HEAD_DIM), eye2=jnp.asarray(eye2))


RWKV_SEQS_PER_STEP = 4
_RWKV_W_ORDER = ('mu', 'w0', 'a0', 'wwa', 'g2', 'kk', 'ka', 'rk', 'lnw', 'lnb', 'p64', 'eye2')


def rwkv_mix(zrw, prev, state0, wts, tile_l=256):
    bsz, length, cols = zrw.shape
    tile_l = min(tile_l, length)
    nb = math.gcd(bsz, RWKV_SEQS_PER_STEP)
    assert length % tile_l == 0
    consts = [wts[k] for k in _RWKV_W_ORDER]
    tile_pad = -(-tile_l // SUBLANES) * SUBLANES
    seq_scratch = [pltpu.VMEM((nb, tile_pad, RWKV_DIM), F32) for _ in range(9)]
    return pl.pallas_call(
        _rwkv_kernel,
        out_shape=[jax.ShapeDtypeStruct((bsz, length, RWKV_DIM), BF16), jax.ShapeDtypeStruct(state0.shape, F32)],
        grid=(bsz // nb, length // tile_l),
        in_specs=[pl.BlockSpec((nb, tile_l, cols), lambda b, j: (b, j, 0)),
                  pl.BlockSpec((nb, 1, cols), lambda b, j: (b, 0, 0)),
                  pl.BlockSpec((nb,) + state0.shape[1:], lambda b, j: (b, 0, 0, 0))]
                 + [_const_spec(c.shape) for c in consts],
        out_specs=[pl.BlockSpec((nb, tile_l, RWKV_DIM), lambda b, j: (b, j, 0)),
                   pl.BlockSpec((nb,) + state0.shape[1:], lambda b, j: (b, 0, 0, 0))],
        scratch_shapes=[pltpu.VMEM((tile_l + 8, cols), F32), pltpu.VMEM((nb, 8, cols), F32),
                        pltpu.VMEM((nb, RWKV_HEADS // 2, HEAD_DIM, 2 * HEAD_DIM), F32)] + seq_scratch,
        compiler_params=_cparams(("parallel", "arbitrary")),
        name="rwkv_mix",
    )(zrw, prev.reshape(bsz, 1, cols), state0, *consts)


def _t5_bucket(dist):
    n = jnp.maximum(dist, 0)
    nf = jnp.maximum(n, 1).astype(F32)
    large = MAX_EXACT + (jnp.log(nf / MAX_EXACT) / math.log(REL_MAX_DIST / MAX_EXACT)
                         * (NUM_BUCKETS - MAX_EXACT)).astype(jnp.int32)
    return jnp.where(n < MAX_EXACT, n, jnp.minimum(large, NUM_BUCKETS - 1))


def _bias_table(rel_bias, dist, valid=None):
    onehot = (_t5_bucket(dist)[..., None] == jnp.arange(NUM_BUCKETS)).astype(F32)
    b = jnp.einsum('...k,kh->...h', onehot, rel_bias.astype(F32), precision=lax.Precision.HIGHEST)
    if valid is not None:
        b = jnp.where(valid[..., None], b, NEG_BIG)
    return jnp.moveaxis(b, -1, 0)


def _flash_init(m_sc, l_sc, acc_sc):
    m_sc[...] = jnp.full(m_sc.shape, NEG_BIG, F32)
    l_sc[...] = jnp.zeros(l_sc.shape, F32)
    acc_sc[...] = jnp.zeros(acc_sc.shape, F32)


def _flash_update(s, v, valid, m_sc, l_sc, acc_sc, v_feature_major=False):
    if valid is not None:
        s = jnp.where(valid, s, NEG_BIG)
    m_prev = m_sc[...]
    m_new = jnp.maximum(m_prev, jnp.max(s, axis=-1, keepdims=True))
    p = jnp.exp(s - m_new)
    if valid is not None:
        p = jnp.where(valid, p, 0.0)
    alpha = jnp.exp(m_prev - m_new)
    l_sc[...] = alpha * l_sc[...] + jnp.sum(p, axis=-1, keepdims=True)
    pv = _dot_nt(p.astype(BF16), v) if v_feature_major else jnp.dot(p.astype(BF16), v, preferred_element_type=F32)
    acc_sc[...] = alpha * acc_sc[...] + pv
    m_sc[...] = m_new


def _flash_result(l_sc, acc_sc):
    return acc_sc[...] / jnp.maximum(l_sc[...], 1e-30)


def _dot_nt(a, b):
    return lax.dot_general(a, b, (((1,), (1,)), ((), ())), preferred_element_type=F32)


def _gelu_tanh(x):
    return 0.5 * x * (1.0 + jnp.tanh(math.sqrt(2.0 / math.pi) * (x + 0.044715 * (x * x * x))))


def _compress_parts(rows_ref, kv, row0, n_chunks, wbig_ref, pitch=CMP_STRIDE):
    acc = None
    for jp in range(CMP_STRIDE // 2):
        xs = jnp.concatenate([rows_ref[kv, pl.ds(row0 + 2 * jp + u, n_chunks, stride=pitch), :] for u in range(2)],
                             axis=1).astype(BF16)
        t = jnp.dot(xs, wbig_ref[kv, jp], preferred_element_type=F32)
        acc = t if acc is None else acc + t
    return acc


def _compress_finish(part_lo, part_hi, kv, pe_ref, w2_ref):
    outs = []
    for g in range(NSA_GROUPS):
        c0 = g * 2 * CMP_HIDDEN
        pre = part_lo[:, c0:c0 + CMP_HIDDEN] + part_hi[:, c0 + CMP_HIDDEN:c0 + 2 * CMP_HIDDEN] + pe_ref[kv:kv + 1, :]
        outs.append(jnp.dot(_gelu_tanh(pre).astype(BF16), w2_ref[kv], preferred_element_type=F32))
    return outs


def _prep_cmp_weights(p):
    w1 = p['cmp_w1'].reshape(2, 2, CMP_STRIDE, HEAD_DIM, CMP_HIDDEN)
    per_j = jnp.concatenate([w1[:, 0], w1[:, 1]], axis=-1)
    z = jnp.zeros_like(per_j)
    wbig = jnp.concatenate([jnp.concatenate([per_j, z], axis=-1), jnp.concatenate([z, per_j], axis=-1)], axis=2)
    wbig = wbig.reshape(2, CMP_STRIDE // 2, 2 * NSA_GROUPS * HEAD_DIM, 2 * NSA_GROUPS * CMP_HIDDEN)
    pe = jnp.transpose(p['cmp_pe'], (1, 0, 2)).reshape(2, CMP_LEN * HEAD_DIM)
    pe_term = jnp.einsum('kf,kfh->kh', pe, p['cmp_w1'], precision=lax.Precision.HIGHEST)
    return dict(wbig=wbig.astype(BF16), pe=pe_term.astype(F32), w2=p['cmp_w2'].astype(BF16),
                kn0=p['nsa_k_norm'][0].reshape(1, HEAD_DIM).astype(F32))


def _overlap_matrix(n_cmp, n_slc, rows, cols):
    cstart = np.arange(rows) * CMP_STRIDE
    sstart = np.arange(cols) * SEL_BLOCK
    m = ((cstart[:, None] < sstart[None, :] + SEL_BLOCK) & (cstart[:, None] + CMP_LEN > sstart[None, :])
         & (np.arange(rows)[:, None] < n_cmp) & (np.arange(cols)[None, :] < n_slc))
    return m.astype(np.float32)


def _block_ranks(score, qpos, n_slc):
    lane = lax.broadcasted_iota(jnp.int32, score.shape, 1)
    cur = lax.shift_right_logical(qpos, int(math.log2(SEL_BLOCK)))
    forced = (lane == 0) | (lane == cur) | (lane == cur - 1)
    s = jnp.where(forced, FORCED_SCORE, jnp.where(lane <= cur, score, -jnp.inf))
    s = jnp.where(lane < n_slc, s, -jnp.inf)
    rank = jnp.zeros(score.shape, jnp.int32)
    for i in range(n_slc):
        si = s[:, i:i + 1]
        rank = rank + jnp.where((si > s) | ((si == s) & (i < lane)), 1, 0)
    return jnp.where(lane < n_slc, rank, n_slc)


def _select_blocks(score, qpos, n_slc, n_top):
    return _block_ranks(score, qpos, n_slc) < n_top


def _nsa_cmp_prompt_kernel(rows_ref, q_ref, bias_ref, wbig_ref, pe_ref, w2_ref, kn0_ref, ovl_ref,
                           o_ref, sel_ref, part_sc, kc_sc, vc_sc, *, n_slc, n_top):
    qt = pl.program_id(1)
    tq = q_ref.shape[1]
    n_chunks = kc_sc.shape[1]

    @pl.when(qt == 0)
    def _():
        part_sc[...] = jnp.zeros(part_sc.shape, F32)
        for kv in range(2):
            part_sc[kv, 0:n_chunks, :] = _compress_parts(rows_ref.at[0], kv, 0, n_chunks, wbig_ref)
        for kv in range(2):
            summ = _compress_finish(part_sc[kv, 0:n_chunks, :], part_sc[kv, 1:n_chunks + 1, :], kv, pe_ref, w2_ref)
            for g in range(NSA_GROUPS):
                if kv == 0:
                    kn = summ[g] * lax.rsqrt(jnp.mean(summ[g] * summ[g], axis=-1, keepdims=True) + NORM_EPS) * kn0_ref[...]
                    kc_sc[g] = kn.astype(BF16)
                else:
                    vc_sc[g] = summ[g].astype(BF16)

    qpos = qt * tq + lax.broadcasted_iota(jnp.int32, (tq, 1), 0)
    for g in range(NSA_GROUPS):
        imp = jnp.zeros((tq, n_chunks), F32)
        for hh in range(NSA_HPG):
            h = g * NSA_HPG + hh
            z = _dot_nt(q_ref[0, :, h * HEAD_DIM:(h + 1) * HEAD_DIM], kc_sc[g]) + bias_ref[h]
            valid = bias_ref[h] > 0.5 * NEG_BIG
            m = jnp.max(z, axis=-1, keepdims=True)
            e = jnp.where(valid, jnp.exp(z - m), 0.0)
            pr = e / jnp.maximum(jnp.sum(e, axis=-1, keepdims=True), 1e-30)
            o_ref[0, :, h * HEAD_DIM:(h + 1) * HEAD_DIM] = jnp.dot(pr.astype(BF16), vc_sc[g], preferred_element_type=F32)
            imp = imp + pr
        score = _gsum_hl(imp, ovl_ref[...])
        sel_ref[0, g] = jnp.where(_select_blocks(score, qpos, n_slc, n_top), 1.0, 0.0).astype(sel_ref.dtype)


def nsa_cmp_prompt(kv_cmp, qn, rel_bias, cw, tile_q=256):
    bsz, seq, _ = kv_cmp.shape
    tile_q = min(tile_q, seq)
    n_chunks = seq // CMP_STRIDE
    n_cmp = n_chunks - CMP_LEN // CMP_STRIDE + 1
    n_slc = seq // SEL_BLOCK
    assert n_chunks % 128 == 0 and n_slc <= 128
    half = NSA_GROUPS * HEAD_DIM
    rows = jnp.transpose(kv_cmp.reshape(bsz, seq, 2, half), (0, 2, 1, 3))
    q_pos = jnp.arange(seq)[:, None]
    n_idx = jnp.arange(n_chunks)[None, :]
    dist = q_pos - (n_idx * CMP_STRIDE + CMP_LEN - 1)
    bias = _bias_table(rel_bias, dist, (dist >= 0) & (n_idx < n_cmp))
    ovl = jnp.asarray(_overlap_matrix(n_cmp, n_slc, n_chunks, 128), dtype=BF16)
    consts = [cw['wbig'], cw['pe'], cw['w2'], cw['kn0'], ovl]
    kern = functools.partial(_nsa_cmp_prompt_kernel, n_slc=n_slc, n_top=min(TOP_N, n_slc))
    return pl.pallas_call(
        kern,
        out_shape=[jax.ShapeDtypeStruct((bsz, seq, NSA_DIM), F32),
                   jax.ShapeDtypeStruct((bsz, NSA_GROUPS, seq, 128), BF16)],
        grid=(bsz, seq // tile_q),
        in_specs=[pl.BlockSpec((1, 2, seq, half), lambda b, i: (b, 0, 0, 0)),
                  pl.BlockSpec((1, tile_q, NSA_DIM), lambda b, i: (b, i, 0)),
                  pl.BlockSpec((NSA_HEADS, tile_q, n_chunks), lambda b, i: (0, i, 0))]
                 + [_const_spec(c.shape) for c in consts],
        out_specs=[pl.BlockSpec((1, tile_q, NSA_DIM), lambda b, i: (b, i, 0)),
                   pl.BlockSpec((1, NSA_GROUPS, tile_q, 128), lambda b, i: (b, 0, i, 0))],
        scratch_shapes=[pltpu.VMEM((2, n_chunks + 8, 2 * NSA_GROUPS * CMP_HIDDEN), F32),
                        pltpu.VMEM((NSA_GROUPS, n_chunks, HEAD_DIM), BF16),
                        pltpu.VMEM((NSA_GROUPS, n_chunks, HEAD_DIM), BF16)],
        compiler_params=_cparams(("parallel", "arbitrary")),
        name="nsa_cmp_prompt",
    )(rows, qn, bias, *consts)


def _flash_aug_update(s, v_aug, m_sc, acc_sc):
    lanes = m_sc.shape[1]
    rep = lambda x, w: x if w == lanes else jnp.concatenate([x] * (w // lanes), axis=1)
    m_prev = m_sc[...]
    m_new = jnp.maximum(m_prev, jnp.max(s, axis=-1, keepdims=True))
    p = jnp.exp(s - rep(m_new, s.shape[1]))
    acc_sc[...] = (rep(jnp.exp(m_prev - m_new), acc_sc.shape[1]) * acc_sc[...]
                   + jnp.dot(p.astype(BF16), v_aug, preferred_element_type=F32))
    m_sc[...] = m_new


def _nsa_prompt_kernel(q_ref, kvs_ref, kvw_ref, sel_ref, ocmp_ref, gates_ref, btile_ref, eg_ref,
                       o_ref, ks_sc, kw_sc, vs_sc, vw_sc, m_sc, acc_sc):
    qt = pl.program_id(1)
    tq = q_ref.shape[1]
    tk = tq
    half = NSA_GROUPS * HEAD_DIM

    @pl.when(qt == 0)
    def _():
        lane = lax.broadcasted_iota(jnp.int32, (kvs_ref.shape[1], half), 1)
        for kv_ref, k_sc, v_sc in ((kvs_ref, ks_sc, vs_sc), (kvw_ref, kw_sc, vw_sc)):
            k_sc[...] = kv_ref[0, :, 0:half].astype(BF16)
            v = kv_ref[0, :, half:2 * half]
            v_sc[0] = jnp.where(lane < HEAD_DIM, v, 1.0).astype(BF16)
            v_sc[1] = jnp.where(lane < HEAD_DIM, pltpu.roll(v, HEAD_DIM, axis=1), 1.0).astype(BF16)

    ri = lax.broadcasted_iota(jnp.int32, (tq, tk), 0)
    ci = lax.broadcasted_iota(jnp.int32, (tq, tk), 1)
    causal_add = jnp.where(ci <= ri, 0.0, NEG_BIG)
    upper_add = jnp.where(ci > ri, 0.0, NEG_BIG)
    stack = lambda x: jnp.concatenate([x] * NSA_HPG, axis=0)
    blocks_per_tile = tk // SEL_BLOCK
    lane_q = lax.broadcasted_iota(jnp.int32, (tq, half), 1)

    gates = gates_ref[0]
    g_hi = gates.astype(BF16)
    g_lo = (gates - g_hi.astype(F32)).astype(BF16)
    gexp = (jnp.dot(g_hi, eg_ref[...], preferred_element_type=F32) + jnp.dot(g_lo, eg_ref[...], preferred_element_type=F32))

    o_sel_heads, o_win_heads = [], []
    for g in range(NSA_GROUPS):
        heads = [g * NSA_HPG + hh for hh in range(NSA_HPG)]
        pieces = []
        for h in heads:
            q2 = q_ref[0, :, (h // 2) * half:(h // 2 + 1) * half].astype(F32)
            q2 = q2 if h % 2 == g else pltpu.roll(q2, HEAD_DIM, axis=1)
            pieces.append(jnp.where((lane_q < HEAD_DIM) == (g == 0), q2, 0.0).astype(BF16))
        qs = jnp.concatenate(pieces, axis=0)
        bias_d0 = jnp.concatenate([btile_ref[h, 0] for h in heads], axis=0)
        bias_d1 = jnp.concatenate([btile_ref[h, 1] for h in heads], axis=0)
        selg = sel_ref[0, g]

        def tile(k_sc, v_sc, kt, add, n_tiles=1):
            r0 = pl.multiple_of(kt * tk, tk)
            s = _dot_nt(qs, k_sc[pl.ds(r0, n_tiles * tk), :])
            if add is not None:
                s = s + add
            _flash_aug_update(s, v_sc[g, pl.ds(r0, n_tiles * tk), :], m_sc, acc_sc)

        def sel_add(kt, n_tiles=1):
            blk_i = lax.broadcasted_iota(jnp.int32, (128, n_tiles * tk), 0)
            blk_of_col = lax.shift_right_logical(lax.broadcasted_iota(jnp.int32, (128, n_tiles * tk), 1),
                                                 int(math.log2(SEL_BLOCK)))
            e = jnp.where(blk_i == kt * blocks_per_tile + blk_of_col, 1.0, 0.0).astype(BF16)
            return (jnp.dot(selg, e, preferred_element_type=F32) - 1.0) * (-NEG_BIG)

        def init():
            m_sc[...] = jnp.full(m_sc.shape, NEG_BIG, F32)
            acc_sc[...] = jnp.zeros(acc_sc.shape, F32)

        def result():
            acc = acc_sc[...]
            return acc[:, 0:HEAD_DIM] / acc[:, HEAD_DIM:2 * HEAD_DIM]

        init()
        tile(ks_sc, vs_sc, qt, bias_d0 + stack(causal_add + sel_add(qt)))

        @pl.when(qt >= 1)
        def _():
            tile(ks_sc, vs_sc, qt - 1, bias_d1 + stack(sel_add(qt - 1)))

        n_far = jnp.maximum(qt - 1, 0)

        def sel_far(i, c):
            tile(ks_sc, vs_sc, 2 * i, stack(sel_add(2 * i, 2)), 2)
            return c

        lax.fori_loop(0, n_far // 2, sel_far, 0)

        @pl.when(lax.rem(n_far, 2) == 1)
        def _():
            tile(ks_sc, vs_sc, n_far - 1, stack(sel_add(n_far - 1)))

        o_sel = result()

        init()
        n_back = WINDOW // tk
        assert n_back == 4
        tile(kw_sc, vw_sc, qt, bias_d0 + stack(causal_add))

        @pl.when(qt >= 1)
        def _():
            tile(kw_sc, vw_sc, qt - 1, bias_d1)

        @pl.when(qt >= 3)
        def _():
            tile(kw_sc, vw_sc, qt - 3, None, 2)

        @pl.when(qt == 2)
        def _():
            tile(kw_sc, vw_sc, 0, None)

        @pl.when(qt >= n_back)
        def _():
            tile(kw_sc, vw_sc, qt - n_back, stack(upper_add))

        o_win = result()
        for hh in range(NSA_HPG):
            o_sel_heads.append(o_sel[hh * tq:(hh + 1) * tq, :])
            o_win_heads.append(o_win[hh * tq:(hh + 1) * tq, :])

    o_sel = jnp.concatenate(o_sel_heads, axis=1)
    o_win = jnp.concatenate(o_win_heads, axis=1)
    o = (gexp[:, 0:NSA_DIM] * ocmp_ref[0] + gexp[:, NSA_DIM:2 * NSA_DIM] * o_sel
         + gexp[:, 2 * NSA_DIM:3 * NSA_DIM] * o_win)
    o_ref[0] = o.astype(o_ref.dtype)


def _gate_expand_matrix():
    m = np.zeros((128, NSA_BRANCHES * NSA_DIM), np.float32)
    for c in range(NSA_BRANCHES):
        for h in range(NSA_HEADS):
            m[c * NSA_HEADS + h, c * NSA_DIM + h * HEAD_DIM:c * NSA_DIM + (h + 1) * HEAD_DIM] = 1.0
    return jnp.asarray(m, dtype=BF16)


def nsa_prompt(qn, kv_sel, kv_win, selmask, o_cmp, gates, rel_bias, tile=128):
    bsz, seq, _ = qn.shape
    assert seq % tile == 0 and WINDOW % tile == 0 and tile % SEL_BLOCK == 0
    i = jnp.arange(tile)
    d0 = i[:, None] - i[None, :]
    far = rel_bias[NUM_BUCKETS - 1].astype(F32)[:, None, None, None]
    btile = jnp.stack([_bias_table(rel_bias, d0), _bias_table(rel_bias, d0 + tile)], axis=1) - far
    eg = _gate_expand_matrix()
    rows = NSA_HPG * tile
    half = NSA_GROUPS * HEAD_DIM
    return pl.pallas_call(
        _nsa_prompt_kernel,
        out_shape=jax.ShapeDtypeStruct((bsz, seq, NSA_DIM), BF16),
        grid=(bsz, seq // tile),
        in_specs=[pl.BlockSpec((1, tile, NSA_DIM), lambda b, i: (b, i, 0)),
                  pl.BlockSpec((1, seq, NSA_KV_COLS), lambda b, i: (b, 0, 0)),
                  pl.BlockSpec((1, seq, NSA_KV_COLS), lambda b, i: (b, 0, 0)),
                  pl.BlockSpec((1, NSA_GROUPS, tile, 128), lambda b, i: (b, 0, i, 0)),
                  pl.BlockSpec((1, tile, NSA_DIM), lambda b, i: (b, i, 0)),
                  pl.BlockSpec((1, tile, 128), lambda b, i: (b, i, 0)),
                  _const_spec(btile.shape), _const_spec(eg.shape)],
        out_specs=pl.BlockSpec((1, tile, NSA_DIM), lambda b, i: (b, i, 0)),
        scratch_shapes=[pltpu.VMEM((seq, half), BF16), pltpu.VMEM((seq, half), BF16),
                        pltpu.VMEM((NSA_GROUPS, seq, half), BF16), pltpu.VMEM((NSA_GROUPS, seq, half), BF16),
                        pltpu.VMEM((rows, half), F32), pltpu.VMEM((rows, half), F32)],
        compiler_params=_cparams(("parallel", "arbitrary")),
        name="nsa_prompt",
    )(qn, kv_sel, kv_win, selmask, o_cmp, gates, btile, eg)


def _mla_prompt_kernel(q_ref, k_ref, lat_ref, wuv_ref, o_ref, caug_sc, m_sc, acc_sc):
    qt = pl.program_id(1)
    tq = q_ref.shape[1]
    tk = tq

    @pl.when(qt == 0)
    def _():
        caug_sc[:, 0:KV_LORA] = lat_ref[0, :, 0:KV_LORA].astype(BF16)
        caug_sc[:, KV_LORA:2 * KV_LORA] = jnp.ones((caug_sc.shape[0], KV_LORA), BF16)

    ri = lax.broadcasted_iota(jnp.int32, (tq, tk), 0)
    ci = lax.broadcasted_iota(jnp.int32, (tq, tk), 1)
    causal_add = jnp.concatenate([jnp.where(ci <= ri, 0.0, NEG_BIG)] * MLA_HEADS, axis=0)
    qh = [q_ref[0, :, h * MLA_QK_PAD:(h + 1) * MLA_QK_PAD] for h in range(MLA_HEADS)]

    def tile(kt, add, n_tiles=1):
        r0 = pl.multiple_of(kt * tk, tk)
        rows = pl.ds(r0, n_tiles * tk)
        s = jnp.concatenate([_dot_nt(qh[h], k_ref[0, rows, h * MLA_QK_PAD:(h + 1) * MLA_QK_PAD])
                             for h in range(MLA_HEADS)], axis=0)
        if add is not None:
            s = s + add
        _flash_aug_update(s, caug_sc[rows, :], m_sc, acc_sc)

    m_sc[...] = jnp.full(m_sc.shape, NEG_BIG, F32)
    acc_sc[...] = jnp.zeros(acc_sc.shape, F32)
    tile(qt, causal_add)

    def body(i, c):
        tile(2 * i, None, 2)
        return c

    lax.fori_loop(0, qt // 2, body, 0)

    @pl.when(lax.rem(qt, 2) == 1)
    def _():
        tile(qt - 1, None)

    acc = acc_sc[...]
    o_lat = (acc[:, 0:KV_LORA] / acc[:, KV_LORA:2 * KV_LORA]).astype(BF16)
    o_ref[0] = jnp.concatenate(
        [jnp.dot(o_lat[h * tq:(h + 1) * tq, :], wuv_ref[:, h * V_DIM:(h + 1) * V_DIM], preferred_element_type=F32)
         for h in range(MLA_HEADS)], axis=1).astype(o_ref.dtype)


def mla_prompt(qm, km, lat, w_uv, tile=128):
    bsz, seq, width = qm.shape
    assert seq % tile == 0
    wuv = w_uv.reshape(KV_LORA, MLA_DIM).astype(BF16)
    rows = MLA_HEADS * tile
    return pl.pallas_call(
        _mla_prompt_kernel,
        out_shape=jax.ShapeDtypeStruct((bsz, seq, MLA_DIM), BF16),
        grid=(bsz, seq // tile),
        in_specs=[pl.BlockSpec((1, tile, width), lambda b, i: (b, i, 0)),
                  pl.BlockSpec((1, seq, width), lambda b, i: (b, 0, 0)),
                  pl.BlockSpec((1, seq, KV_LORA + ROPE_DIM), lambda b, i: (b, 0, 0)),
                  _const_spec(wuv.shape)],
        out_specs=pl.BlockSpec((1, tile, MLA_DIM), lambda b, i: (b, i, 0)),
        scratch_shapes=[pltpu.VMEM((seq, 2 * KV_LORA), BF16), pltpu.VMEM((rows, KV_LORA), F32),
                        pltpu.VMEM((rows, 2 * KV_LORA), F32)],
        compiler_params=_cparams(("parallel", "arbitrary")),
        name="mla_prompt",
    )(qm, km, lat, wuv)


ROUTE_GROUP_LANE0 = N_EXPERTS


def _merge_kernel(x_ref, oa_ref, ob_ref, oc_ref, gm_ref, wa_ref, wb_ref, wc_ref, wo_ref, nf_ref, wrh_ref, wrl_ref,
                  br_ref, x1_ref, xn_ref, route_ref):
    d = x_ref.shape[1]
    dot = lambda a, w_ref: jnp.dot(a, w_ref[...], preferred_element_type=F32)
    h = (gm_ref[:, 0:d].astype(F32) * dot(oa_ref[...], wa_ref)
         + gm_ref[:, d:2 * d].astype(F32) * dot(ob_ref[...], wb_ref)
         + gm_ref[:, 2 * d:3 * d].astype(F32) * dot(oc_ref[...], wc_ref))
    x1 = x_ref[...] + dot(h.astype(BF16), wo_ref)
    x1_ref[...] = x1
    t = x1 * lax.rsqrt(jnp.mean(x1 * x1, axis=-1, keepdims=True) + NORM_EPS) * nf_ref[...]
    xn_ref[...] = t
    t_hi = t.astype(BF16)
    t_lo = (t - t_hi.astype(F32)).astype(BF16)
    logits = dot(t_hi, wrh_ref) + dot(t_lo, wrh_ref) + dot(t_hi, wrl_ref) + br_ref[...]

    lane = lax.broadcasted_iota(jnp.int32, logits.shape, 1)
    big = jnp.int32(1 << 20)
    rmax = lambda a: jnp.max(a, axis=-1, keepdims=True)
    first = lambda hit: jnp.min(jnp.where(hit, lane, big), axis=-1, keepdims=True)
    is_g = (lane >= ROUTE_GROUP_LANE0) & (lane < ROUTE_GROUP_LANE0 + N_GROUPS)
    gl = jnp.where(is_g, logits, -jnp.inf)
    gmax = rmax(gl)
    gi = first(gl == gmax) - ROUTE_GROUP_LANE0
    gw = 1.0 / jnp.sum(jnp.where(is_g, jnp.exp(gl - gmax), 0.0), axis=-1, keepdims=True)
    in_group = (lane < N_EXPERTS) & (lax.shift_right_logical(lane, int(math.log2(EXPERTS_PER_GROUP))) == gi)
    el = jnp.where(in_group, logits, -jnp.inf)
    e1 = rmax(el)
    i1 = first(el == e1)
    el2 = jnp.where(lane == i1, -jnp.inf, el)
    e2 = rmax(el2)
    i2 = first(el2 == e2)
    r = jnp.exp(e2 - e1)
    w1 = gw / (1.0 + r)
    w2 = gw * r / (1.0 + r)
    route_ref[...] = jnp.where(lane == 0, i1.astype(F32), jnp.where(lane == 1, i2.astype(F32),
                               jnp.where(lane == 2, w1, jnp.where(lane == 3, w2, 0.0))))


def _prep_merge_weights(p):
    wr = jnp.concatenate([p['w_router_expert'], p['w_router_group']], axis=1)
    wr = jnp.pad(wr, ((0, 0), (0, 128 - wr.shape[1]))).astype(F32)
    wr_hi = wr.astype(BF16)
    wr_lo = (wr - wr_hi.astype(F32)).astype(BF16)
    br = jnp.pad(jnp.concatenate([p['b_router_expert'], p['b_router_group']]), (0, 128 - N_EXPERTS - N_GROUPS))
    return dict(wa=p['w_br_a'].astype(BF16), wb=p['w_br_b'].astype(BF16), wc=p['w_br_c'].astype(BF16),
                wo=p['w_out'].astype(BF16), nf=p['norm_ffn'].reshape(1, -1).astype(F32), wr_hi=wr_hi, wr_lo=wr_lo,
                br=br.reshape(1, -1).astype(F32))


_MERGE_W_ORDER = ('wa', 'wb', 'wc', 'wo', 'nf', 'wr_hi', 'wr_lo', 'br')


def merge_route(x, o_a, o_b, o_c, gm, wts, tile_m=256):
    t, d = x.shape
    tile_m = min(tile_m, t)
    assert t % tile_m == 0
    consts = [wts[k] for k in _MERGE_W_ORDER]
    row = lambda w: pl.BlockSpec((tile_m, w), lambda i: (i, 0))
    return pl.pallas_call(
        _merge_kernel,
        out_shape=[jax.ShapeDtypeStruct((t, d), F32), jax.ShapeDtypeStruct((t, d), F32),
                   jax.ShapeDtypeStruct((t, 128), F32)],
        grid=(t // tile_m,),
        in_specs=[row(d), row(o_a.shape[1]), row(o_b.shape[1]), row(o_c.shape[1]), row(gm.shape[1])]
                 + [_const_spec(c.shape) for c in consts],
        out_specs=[row(d), row(d), row(128)],
        compiler_params=_cparams(("parallel",)),
        name="merge_route",
    )(x, o_a, o_b, o_c, gm, *consts)


MOE_TILE = 128


def _moe_dispatch(eid):
    t = eid.shape[0]
    n_pairs = 2 * t
    n_tiles = -(-n_pairs // MOE_TILE) + N_EXPERTS
    flat = eid.reshape(-1)
    onehot = (flat[:, None] == jnp.arange(N_EXPERTS, dtype=jnp.int32)[None, :]).astype(jnp.int32)
    csum = jnp.cumsum(onehot, axis=0)
    counts = csum[-1]
    rank = jnp.sum((csum - onehot) * onehot, axis=1)
    padded = ((counts + MOE_TILE - 1) // MOE_TILE) * MOE_TILE
    seg_end = jnp.cumsum(padded)
    seg_start = seg_end - padded
    first_of_e = jnp.cumsum(counts) - counts
    slot_of_pair = jnp.sum(onehot * seg_start[None, :], axis=1) + rank
    tile_start = jnp.arange(n_tiles, dtype=jnp.int32) * MOE_TILE
    tile_expert = jnp.minimum(jnp.sum((tile_start[:, None] >= seg_end[None, :]).astype(jnp.int32), axis=1),
                              N_EXPERTS - 1)
    order = jnp.argsort(flat, stable=True).astype(jnp.int32)
    slot = jnp.arange(n_tiles * MOE_TILE, dtype=jnp.int32)
    e_of_slot = jnp.repeat(tile_expert, MOE_TILE)
    q = jnp.clip(first_of_e[e_of_slot] + slot - seg_start[e_of_slot], 0, n_pairs - 1)
    tok_of_slot = order[q] // 2
    return tok_of_slot.reshape(n_tiles, 1, MOE_TILE), slot_of_pair.reshape(t, 2), tile_expert, n_tiles


ROW_GATHER_UNROLL = 8


def _row_gather(src_ref, idx_ref, n_rows, dst_ref, sem, start):
    if not start:
        pltpu.make_async_copy(src_ref.at[pl.ds(0, n_rows), :], dst_ref, sem).wait()
        return

    def body(r0, c):
        for u in range(ROW_GATHER_UNROLL):
            r = r0 * ROW_GATHER_UNROLL + u
            pltpu.make_async_copy(src_ref.at[pl.ds(idx_ref[0, 0, r], 1), :], dst_ref.at[pl.ds(r, 1), :],
                                  sem).start(priority=u % 2)
        return c

    lax.fori_loop(0, n_rows // ROW_GATHER_UNROLL, body, 0)


def _moe_gemm_kernel(te_ref, tok_ref, tok_next_ref, xn_ref, wg_ref, wu_ref, wd_ref, y_ref, xbuf, sems,
                     wg_sc, wu_sc, wd_sc):
    i = pl.program_id(0)
    n = pl.num_programs(0)
    slot = lax.rem(i, 2)

    @pl.when(i == 0)
    def _():
        _row_gather(xn_ref, tok_ref, MOE_TILE, xbuf.at[0], sems.at[0], True)

    @pl.when(i + 1 < n)
    def _():
        _row_gather(xn_ref, tok_next_ref, MOE_TILE, xbuf.at[1 - slot], sems.at[1 - slot], True)

    @pl.when((i == 0) | (te_ref[i] != te_ref[jnp.maximum(i - 1, 0)]))
    def _():
        wg_sc[...] = wg_ref[0, 0].astype(BF16)
        wu_sc[...] = wu_ref[0, 0].astype(BF16)
        wd_sc[...] = wd_ref[0, 0].astype(BF16)

    _row_gather(xn_ref, tok_ref, MOE_TILE, xbuf.at[slot], sems.at[slot], False)
    xg = xbuf[slot].astype(BF16)
    hg = jnp.dot(xg, wg_sc[...], preferred_element_type=F32)
    hu = jnp.dot(xg, wu_sc[...], preferred_element_type=F32)
    act = (hg * jax.nn.sigmoid(hg)) * hu
    y_ref[...] = jnp.dot(act.astype(BF16), wd_sc[...], preferred_element_type=F32)


def _moe_combine_kernel(pos_ref, pos_next_ref, y_ref, x1_ref, route_ref, o_ref, ybuf, sems):
    i = pl.program_id(0)
    n = pl.num_programs(0)
    slot = lax.rem(i, 2)
    rows = 2 * MOE_TILE

    @pl.when(i == 0)
    def _():
        _row_gather(y_ref, pos_ref, rows, ybuf.at[0], sems.at[0], True)

    @pl.when(i + 1 < n)
    def _():
        _row_gather(y_ref, pos_next_ref, rows, ybuf.at[1 - slot], sems.at[1 - slot], True)

    _row_gather(y_ref, pos_ref, rows, ybuf.at[slot], sems.at[slot], False)
    route = route_ref[...]
    o_ref[...] = (x1_ref[...] + route[:, 2:3] * ybuf[slot, 0:MOE_TILE, :] + route[:, 3:4] * ybuf[slot, MOE_TILE:rows, :])


def moe_ffn(x1, xn, route, w_gate, w_up, w_down, layer):
    t, d = x1.shape
    assert t % MOE_TILE == 0
    eid = route[:, 0:2].astype(jnp.int32)
    tok_of_slot, slot_of_pair, tile_expert, n_tiles = _moe_dispatch(eid)
    ff = w_gate.shape[-1]
    idx_spec = lambda w, off, n: pl.BlockSpec((1, 1, w), lambda i, *_: (jnp.minimum(i + off, n - 1), 0, 0),
                                              memory_space=pltpu.SMEM)
    n_tok_tiles = t // MOE_TILE
    y = pl.pallas_call(
        _moe_gemm_kernel,
        out_shape=jax.ShapeDtypeStruct((n_tiles * MOE_TILE, d), F32),
        grid_spec=pltpu.PrefetchScalarGridSpec(
            num_scalar_prefetch=1,
            grid=(n_tiles,),
            in_specs=[idx_spec(MOE_TILE, 0, n_tiles), idx_spec(MOE_TILE, 1, n_tiles),
                      pl.BlockSpec(memory_space=pl.ANY),
                      pl.BlockSpec((1, 1, d, ff), lambda i, te: (layer, te[i], 0, 0)),
                      pl.BlockSpec((1, 1, d, ff), lambda i, te: (layer, te[i], 0, 0)),
                      pl.BlockSpec((1, 1, ff, d), lambda i, te: (layer, te[i], 0, 0))],
            out_specs=pl.BlockSpec((MOE_TILE, d), lambda i, te: (i, 0)),
            scratch_shapes=[pltpu.VMEM((2, MOE_TILE, d), F32), pltpu.SemaphoreType.DMA((2,)),
                            pltpu.VMEM((d, ff), BF16), pltpu.VMEM((d, ff), BF16), pltpu.VMEM((ff, d), BF16)]),
        compiler_params=_cparams(("arbitrary",)),
        name="moe_gemm",
    )(tile_expert, tok_of_slot, tok_of_slot, xn, w_gate, w_up, w_down)

    pos = jnp.transpose(slot_of_pair.reshape(n_tok_tiles, MOE_TILE, 2), (0, 2, 1)).reshape(n_tok_tiles, 1, 2 * MOE_TILE)
    return pl.pallas_call(
        _moe_combine_kernel,
        out_shape=jax.ShapeDtypeStruct((t, d), F32),
        grid=(n_tok_tiles,),
        in_specs=[idx_spec(2 * MOE_TILE, 0, n_tok_tiles), idx_spec(2 * MOE_TILE, 1, n_tok_tiles),
                  pl.BlockSpec(memory_space=pl.ANY),
                  pl.BlockSpec((MOE_TILE, d), lambda i: (i, 0)),
                  pl.BlockSpec((MOE_TILE, 128), lambda i: (i, 0))],
        out_specs=pl.BlockSpec((MOE_TILE, d), lambda i: (i, 0)),
        scratch_shapes=[pltpu.VMEM((2, 2 * MOE_TILE, d), F32), pltpu.SemaphoreType.DMA((2,))],
        compiler_params=_cparams(("arbitrary",)),
        name="moe_combine",
    )(pos, pos, y, x1, route)


PAGES_PER_STEP = 16
CMP_ROW_PITCH = 20


def _pool_feature_major(pool):
    nd = pool.ndim
    return jnp.transpose(pool, (0, 1) + tuple(range(3, nd)) + (2,))


def _fetch_pages(pool_ref, layer, pt_ref, first_page, n_pages, dst_ref, sem, start):
    for pg in range(n_pages):
        phys = pt_ref[0, 0, first_page + pg]
        cp = pltpu.make_async_copy(pool_ref.at[layer, phys], dst_ref.at[pg], sem)
        if start:
            cp.start()
        else:
            cp.wait()


def _paged_pipeline(fetch, pt_ref, ptn_ref, bufs, sems, n_steps, pages_per_step=PAGES_PER_STEP):
    b, c = pl.program_id(0), pl.program_id(1)
    nb = pl.num_programs(0)
    s = b * n_steps + c
    slot = lax.rem(s, 2)

    @pl.when(s == 0)
    def _():
        fetch(pt_ref, 0, bufs.at[0], sems.at[0], True)

    @pl.when(c + 1 < n_steps)
    def _():
        fetch(pt_ref, (c + 1) * pages_per_step, bufs.at[1 - slot], sems.at[1 - slot], True)

    @pl.when((c + 1 == n_steps) & (b + 1 < nb))
    def _():
        fetch(ptn_ref, 0, bufs.at[1 - slot], sems.at[1 - slot], True)

    fetch(pt_ref, c * pages_per_step, bufs.at[slot], sems.at[slot], False)
    return slot


def _stack_heads(q_ref, g):
    return jnp.concatenate([q_ref[0, :, (g * NSA_HPG + hh) * HEAD_DIM:(g * NSA_HPG + hh + 1) * HEAD_DIM]
                            for hh in range(NSA_HPG)], axis=0)


def _nsa_cmp_sample_kernel(pt_ref, ptn_ref, pool_ref, q_ref, bias_ref, wbig_ref, pe_ref, w2_ref, kn0_ref, ovl_ref,
                           o_ref, idx_ref, pagebuf, sems, rows_sc, part_sc, lg_sc, vc_sc,
                           *, layer, n_steps, n_slc, n_top, qpos0):
    c = pl.program_id(1)
    nq = q_ref.shape[1]
    nck = PAGES_PER_STEP * PAGE_SIZE // CMP_STRIDE
    fetch = lambda pt, first, dst, sem, start: _fetch_pages(pool_ref, layer, pt, first, PAGES_PER_STEP, dst, sem, start)
    slot = _paged_pipeline(fetch, pt_ref, ptn_ref, pagebuf, sems, n_steps)

    @pl.when(c == 0)
    def _():
        part_sc[:, 0:8, :] = jnp.zeros((2, 8, part_sc.shape[2]), F32)

    chunks_per_page = PAGE_SIZE // CMP_STRIDE
    for pg in range(PAGES_PER_STEP):
        for kv in range(2):
            x = pagebuf[slot, pg, kv].T
            for n in range(chunks_per_page):
                r0 = (pg * chunks_per_page + n) * CMP_ROW_PITCH
                rows_sc[kv, r0:r0 + CMP_STRIDE, :] = x[n * CMP_STRIDE:(n + 1) * CMP_STRIDE, :]

    for kv in range(2):
        part_sc[kv, 8:8 + nck, :] = _compress_parts(rows_sc, kv, 0, nck, wbig_ref, pitch=CMP_ROW_PITCH)
    for kv in range(2):
        summ = _compress_finish(part_sc[kv, 7:7 + nck, :], part_sc[kv, 8:8 + nck, :], kv, pe_ref, w2_ref)
        for g in range(NSA_GROUPS):
            if kv == 0:
                kn = summ[g] * lax.rsqrt(jnp.mean(summ[g] * summ[g], axis=-1, keepdims=True) + NORM_EPS) * kn0_ref[...]
                lg_sc[g, c] = _dot_nt(_stack_heads(q_ref, g), kn.astype(BF16))
            else:
                vc_sc[g, pl.ds(pl.multiple_of(c * nck, nck), nck), :] = summ[g].astype(BF16)
    for kv in range(2):
        part_sc[kv, 7:8, :] = part_sc[kv, 7 + nck:8 + nck, :]

    @pl.when(c == n_steps - 1)
    def _():
        qpos = qpos0 + lax.broadcasted_iota(jnp.int32, (nq, 1), 0)
        lane = lax.broadcasted_iota(jnp.int32, (nq, ovl_ref.shape[1]), 1)
        for g in range(NSA_GROUPS):
            bias = bias_ref[g]
            z = lg_sc[g] + bias
            m = jnp.max(jnp.max(z, axis=2, keepdims=True), axis=0, keepdims=True)
            e = jnp.where(bias > 0.5 * NEG_BIG, jnp.exp(z - m), 0.0)
            den = jnp.sum(jnp.sum(e, axis=2, keepdims=True), axis=0, keepdims=True)
            pr = e / jnp.maximum(den, 1e-30)
            o_g = jnp.zeros((NSA_HPG * nq, HEAD_DIM), F32)
            score = jnp.zeros((nq, ovl_ref.shape[1]), F32)
            for cc in range(n_steps):
                o_g = o_g + jnp.dot(pr[cc].astype(BF16), vc_sc[g, cc * nck:(cc + 1) * nck, :], preferred_element_type=F32)
                imp = sum(pr[cc][hh * nq:(hh + 1) * nq, :] for hh in range(NSA_HPG))
                score = score + _gsum_hl(imp, ovl_ref[cc * nck:(cc + 1) * nck, :])
            rank = _block_ranks(score, qpos, n_slc)
            idx = jnp.zeros((nq, 128), jnp.int32)
            lane128 = lax.broadcasted_iota(jnp.int32, (nq, 128), 1)
            for r in range(n_top):
                pick = jnp.sum(jnp.where(rank == r, lane, 0), axis=-1, keepdims=True)
                idx = jnp.where(lane128 == r, pick, idx)
            idx_ref[0, g] = idx
            for hh in range(NSA_HPG):
                h = g * NSA_HPG + hh
                o_ref[0, :, h * HEAD_DIM:(h + 1) * HEAD_DIM] = o_g[hh * nq:(hh + 1) * nq, :]


def _sample_rows(t_heads_q):
    h, q = t_heads_q.shape[:2]
    return t_heads_q.reshape((NSA_GROUPS, NSA_HPG * q) + t_heads_q.shape[2:])


def nsa_cmp_sample(pool, layer, page_table, qn, rel_bias, cw):
    bsz, n_pages = page_table.shape
    nq = qn.shape[1]
    past = n_pages * PAGE_SIZE
    assert n_pages % PAGES_PER_STEP == 0
    n_steps = n_pages // PAGES_PER_STEP
    nck = PAGES_PER_STEP * PAGE_SIZE // CMP_STRIDE
    n_keys = n_steps * nck
    n_chunks = (past + nq) // CMP_STRIDE
    n_cmp = n_chunks - CMP_LEN // CMP_STRIDE + 1
    assert n_chunks * CMP_STRIDE <= past + CMP_STRIDE - 1 and n_chunks <= n_keys
    nbp = past // SEL_BLOCK
    n_slc = nbp + -(-nq // SEL_BLOCK)
    width = -(-n_slc // 128) * 128
    qpos = past + jnp.arange(nq)
    n_of_key = jnp.arange(n_keys) - 1
    dist = qpos[:, None] - (n_of_key[None, :] * CMP_STRIDE + CMP_LEN - 1)
    bias = _bias_table(rel_bias, dist, (dist >= 0) & (n_of_key[None, :] >= 0) & (n_of_key[None, :] < n_cmp))
    bias = jnp.transpose(_sample_rows(bias).reshape(NSA_GROUPS, NSA_HPG * nq, n_steps, nck), (0, 2, 1, 3))
    ovl = np.zeros((n_keys, width), np.float32)
    ovl[1:] = _overlap_matrix(n_cmp, n_slc, n_keys - 1, width)
    ovl = jnp.asarray(ovl, dtype=BF16)
    consts = [cw['wbig'], cw['pe'], cw['w2'], cw['kn0'], ovl]
    half = NSA_GROUPS * HEAD_DIM
    pool3 = _pool_feature_major(pool).reshape(pool.shape[0], pool.shape[1], 2, half, PAGE_SIZE)
    pt3 = page_table.reshape(bsz, 1, n_pages)
    kern = functools.partial(_nsa_cmp_sample_kernel, layer=layer, n_steps=n_steps, n_slc=n_slc,
                             n_top=min(TOP_N, n_slc), qpos0=past)
    smem_pt = lambda off: pl.BlockSpec((1, 1, n_pages), lambda b, c: (jnp.minimum(b + off, bsz - 1), 0, 0),
                                       memory_space=pltpu.SMEM)
    return pl.pallas_call(
        kern,
        out_shape=[jax.ShapeDtypeStruct((bsz, nq, NSA_DIM), F32),
                   jax.ShapeDtypeStruct((bsz, NSA_GROUPS, nq, 128), jnp.int32)],
        grid=(bsz, n_steps),
        in_specs=[smem_pt(0), smem_pt(1), pl.BlockSpec(memory_space=pl.ANY),
                  pl.BlockSpec((1, nq, NSA_DIM), lambda b, c: (b, 0, 0)), _const_spec(bias.shape)]
                 + [_const_spec(x.shape) for x in consts],
        out_specs=[pl.BlockSpec((1, nq, NSA_DIM), lambda b, c: (b, 0, 0)),
                   pl.BlockSpec((1, NSA_GROUPS, nq, 128), lambda b, c: (b, 0, 0, 0))],
        scratch_shapes=[pltpu.VMEM((2, PAGES_PER_STEP, 2, half, PAGE_SIZE), F32), pltpu.SemaphoreType.DMA((2,)),
                        pltpu.VMEM((2, nck * CMP_ROW_PITCH, half), F32),
                        pltpu.VMEM((2, nck + 8, 2 * NSA_GROUPS * CMP_HIDDEN), F32),
                        pltpu.VMEM((NSA_GROUPS, n_steps, NSA_HPG * nq, nck), F32),
                        pltpu.VMEM((NSA_GROUPS, n_keys, HEAD_DIM), BF16)],
        compiler_params=_cparams(("arbitrary", "arbitrary")),
        name="nsa_cmp_sample",
    )(pt3, pt3, pool3, qn, bias, *consts)


def _fetch_sel_blocks(pool_ref, layer, idx_ref, pt_ref, nbp, n_top, nq, dst_ref, sem, start):
    bpp = PAGE_SIZE // SEL_BLOCK
    for g in range(NSA_GROUPS):
        for t in range(nq):
            def body(r, c):
                jp = jnp.minimum(idx_ref[0, g, t, r], nbp - 1)
                phys = pt_ref[0, 0, jp // bpp]
                for kv in range(2):
                    cp = pltpu.make_async_copy(pool_ref.at[layer, phys, kv, g],
                                               dst_ref.at[(g * nq + t) * n_top + r, kv], sem)
                    if start:
                        cp.start()
                    else:
                        cp.wait()
                return c

            lax.fori_loop(0, n_top, body, 0)


def _nsa_sample_kernel(idx_ref, idxn_ref, pt_ref, ptn_ref, pool_ref, q_ref, kvs_ref, kvw_ref, win_ref, ocmp_ref,
                       gates_ref, near_ref, far_ref, wb1_ref, wb2_ref, eg_ref, eye_ref,
                       o_ref, nwin_ref, gbuf, sems, osel_sc, owin_sc, *, layer, nbp, n_top):
    b = pl.program_id(0)
    nb = pl.num_programs(0)
    nq = q_ref.shape[1]
    half = NSA_GROUPS * HEAD_DIM
    slot = lax.rem(b, 2)
    fetch = lambda idx, pt, dst, sem, start: _fetch_sel_blocks(pool_ref, layer, idx, pt, nbp, n_top, nq, dst, sem, start)

    @pl.when(b == 0)
    def _():
        fetch(idx_ref, pt_ref, gbuf.at[0], sems.at[0], True)

    @pl.when(b + 1 < nb)
    def _():
        fetch(idxn_ref, ptn_ref, gbuf.at[1 - slot], sems.at[1 - slot], True)

    fetch(idx_ref, pt_ref, gbuf.at[slot], sems.at[slot], False)

    n_near = near_ref.shape[0] // (NSA_GROUPS * nq)
    wbuf = win_ref[0, 0]
    n_new_pad = wb2_ref.shape[2]
    for g in range(NSA_GROUPS):
        kcol = slice(g * HEAD_DIM, (g + 1) * HEAD_DIM)
        vcol = slice(half + g * HEAD_DIM, half + (g + 1) * HEAD_DIM)
        pad_rows = lambda x, n: jnp.concatenate([x, jnp.zeros((n - x.shape[0], x.shape[1]), x.dtype)], axis=0)
        eye_d = eye_ref[0:HEAD_DIM, 0:HEAD_DIM]
        eye_f = eye_d.astype(F32)
        knew = _dot_nt(eye_d, pad_rows(kvs_ref[0, :, kcol], PAGE_SIZE).astype(BF16))
        vnew = _dot_nt(eye_d, pad_rows(kvs_ref[0, :, vcol], PAGE_SIZE).astype(BF16))
        bpp = PAGE_SIZE // SEL_BLOCK
        for t in range(nq):
            ks, vs, bs = [], [], []
            for r in range(n_top):
                idx = idx_ref[0, g, t, r]
                blk = (g * nq + t) * n_top + r
                is_new = idx >= nbp
                ks.append(jnp.where(is_new, knew, gbuf[slot, blk, 0]).astype(BF16))
                vs.append(jnp.where(is_new, vnew, gbuf[slot, blk, 1]).astype(BF16))
                rel = jnp.clip(idx - (nbp + 1 - n_near), 0, n_near - 1)
                near = near_ref[(rel * NSA_GROUPS + g) * nq + t]
                far = far_ref[g * bpp + lax.rem(jnp.minimum(idx, nbp - 1), bpp)]
                bs.append(jnp.where(idx >= nbp + 1 - n_near, near, far))
            bias = jnp.concatenate(bs, axis=1)
            qgt = jnp.concatenate([q_ref[0, t:t + 1, (g * NSA_HPG + hh) * HEAD_DIM:(g * NSA_HPG + hh + 1) * HEAD_DIM]
                                   for hh in range(NSA_HPG)], axis=0)
            z = jnp.dot(qgt, jnp.concatenate(ks, axis=1), preferred_element_type=F32) + bias
            m = jnp.max(z, axis=-1, keepdims=True)
            e = jnp.where(bias > 0.5 * NEG_BIG, jnp.exp(z - m), 0.0)
            pr = e / jnp.maximum(jnp.sum(e, axis=-1, keepdims=True), 1e-30)
            o_t = _dot_nt(jnp.concatenate(vs, axis=1), pr.astype(BF16))
            for hh in range(NSA_HPG):
                h = g * NSA_HPG + hh
                osel_sc[t:t + 1, h * HEAD_DIM:(h + 1) * HEAD_DIM] = jnp.sum(eye_f * o_t[:, hh:hh + 1], axis=0, keepdims=True)
        qs = _stack_heads(q_ref, g)
        b1, b2 = wb1_ref[g], wb2_ref[g]
        z1 = _dot_nt(qs, wbuf[:, kcol].astype(BF16)) + b1
        z2 = _dot_nt(qs, pad_rows(kvw_ref[0, :, kcol], n_new_pad).astype(BF16)) + b2
        m = jnp.maximum(jnp.max(z1, axis=-1, keepdims=True), jnp.max(z2, axis=-1, keepdims=True))
        e1 = jnp.where(b1 > 0.5 * NEG_BIG, jnp.exp(z1 - m), 0.0)
        e2 = jnp.where(b2 > 0.5 * NEG_BIG, jnp.exp(z2 - m), 0.0)
        den = jnp.sum(e1, axis=-1, keepdims=True) + jnp.sum(e2, axis=-1, keepdims=True)
        ow = (jnp.dot(e1.astype(BF16), wbuf[:, vcol].astype(BF16), preferred_element_type=F32)
              + jnp.dot(e2.astype(BF16), pad_rows(kvw_ref[0, :, vcol], n_new_pad).astype(BF16),
                        preferred_element_type=F32)) / jnp.maximum(den, 1e-30)
        for hh in range(NSA_HPG):
            h = g * NSA_HPG + hh
            owin_sc[:, h * HEAD_DIM:(h + 1) * HEAD_DIM] = ow[hh * nq:(hh + 1) * nq, :]

    wlen = wbuf.shape[0]
    nwin_ref[0, 0:wlen - nq, :] = wbuf[nq:wlen, :]
    nwin_ref[0, wlen - nq:wlen, :] = kvw_ref[0]

    gates = gates_ref[0]
    g_hi = gates.astype(BF16)
    g_lo = (gates - g_hi.astype(F32)).astype(BF16)
    gexp = (jnp.dot(g_hi, eg_ref[...], preferred_element_type=F32) + jnp.dot(g_lo, eg_ref[...], preferred_element_type=F32))
    o = (gexp[:, 0:NSA_DIM] * ocmp_ref[0] + gexp[:, NSA_DIM:2 * NSA_DIM] * osel_sc[...]
         + gexp[:, 2 * NSA_DIM:3 * NSA_DIM] * owin_sc[...])
    o_ref[0] = o.astype(o_ref.dtype)


def nsa_sample(pool_sel, win_state, layer, page_table, sel_idx, qn, kv_sel, kv_win, o_cmp, gates, rel_bias):
    bsz, n_pages = page_table.shape
    nq = qn.shape[1]
    past = n_pages * PAGE_SIZE
    nbp = past // SEL_BLOCK
    n_slc = nbp + -(-nq // SEL_BLOCK)
    n_top = min(TOP_N, n_slc)
    wlen = win_state.shape[2]
    assert wlen == WINDOW and past >= WINDOW and nq <= SEL_BLOCK
    qpos = past + jnp.arange(nq)
    n_near = min(n_slc, REL_MAX_DIST // SEL_BLOCK + 1)
    bpp = PAGE_SIZE // SEL_BLOCK
    lane = np.arange(PAGE_SIZE)
    near = []
    for rel in range(n_near):
        blk = n_slc - n_near + rel
        if blk < nbp:
            kpos = (blk // bpp) * PAGE_SIZE + lane
            ok = (lane // SEL_BLOCK) == (blk % bpp)
        else:
            kpos = past + lane
            ok = lane < nq
        dist = qpos[:, None] - jnp.asarray(kpos)[None, :]
        near.append(_bias_table(rel_bias, dist, (dist >= 0) & jnp.asarray(ok)[None, :]))
    near = jnp.stack(near).reshape(n_near, NSA_GROUPS, NSA_HPG, nq, PAGE_SIZE)
    near = jnp.transpose(near, (0, 1, 3, 2, 4)).reshape(n_near * NSA_GROUPS * nq, NSA_HPG, PAGE_SIZE)
    far_h = rel_bias[NUM_BUCKETS - 1].astype(F32).reshape(NSA_GROUPS, 1, NSA_HPG, 1)
    in_half = jnp.asarray((lane[None, :] // SEL_BLOCK) == np.arange(bpp)[:, None]).reshape(1, bpp, 1, PAGE_SIZE)
    far = jnp.where(in_half, far_h, NEG_BIG).reshape(NSA_GROUPS * bpp, NSA_HPG, PAGE_SIZE)
    j = jnp.arange(wlen)
    d1 = qpos[:, None] - (past - wlen + j)[None, :]
    wb1 = _sample_rows(_bias_table(rel_bias, d1, (d1 >= 0) & (d1 < WINDOW)))
    n_new_pad = 128
    tnew = jnp.arange(n_new_pad)
    d2 = qpos[:, None] - (past + tnew)[None, :]
    wb2 = _sample_rows(_bias_table(rel_bias, d2, (d2 >= 0) & (d2 < WINDOW) & (tnew[None, :] < nq)))
    eg = _gate_expand_matrix()
    eye = jnp.asarray(np.eye(128, dtype=np.float32), dtype=BF16)
    consts = [near, far, wb1, wb2, eg, eye]
    pool3 = _pool_feature_major(pool_sel)
    win4 = win_state.reshape(win_state.shape[0], bsz, wlen, NSA_KV_COLS)
    pt3 = page_table.reshape(bsz, 1, n_pages)
    nxt = lambda b: jnp.minimum(b + 1, bsz - 1)
    kern = functools.partial(_nsa_sample_kernel, layer=layer, nbp=nbp, n_top=n_top)
    tok = lambda w: pl.BlockSpec((1, nq, w), lambda b: (b, 0, 0))
    return pl.pallas_call(
        kern,
        out_shape=[jax.ShapeDtypeStruct((bsz, nq, NSA_DIM), BF16), jax.ShapeDtypeStruct((bsz, wlen, NSA_KV_COLS), F32)],
        grid=(bsz,),
        in_specs=[pl.BlockSpec((1, NSA_GROUPS, nq, 128), lambda b: (b, 0, 0, 0), memory_space=pltpu.SMEM),
                  pl.BlockSpec((1, NSA_GROUPS, nq, 128), lambda b: (nxt(b), 0, 0, 0), memory_space=pltpu.SMEM),
                  pl.BlockSpec((1, 1, n_pages), lambda b: (b, 0, 0), memory_space=pltpu.SMEM),
                  pl.BlockSpec((1, 1, n_pages), lambda b: (nxt(b), 0, 0), memory_space=pltpu.SMEM),
                  pl.BlockSpec(memory_space=pl.ANY),
                  tok(NSA_DIM), tok(NSA_KV_COLS), tok(NSA_KV_COLS),
                  pl.BlockSpec((1, 1, wlen, NSA_KV_COLS), lambda b: (layer, b, 0, 0)),
                  tok(NSA_DIM), tok(128)] + [_const_spec(x.shape) for x in consts],
        out_specs=[tok(NSA_DIM), pl.BlockSpec((1, wlen, NSA_KV_COLS), lambda b: (b, 0, 0))],
        scratch_shapes=[pltpu.VMEM((2, NSA_GROUPS * nq * n_top, 2, HEAD_DIM, PAGE_SIZE), F32),
                        pltpu.SemaphoreType.DMA((2,)),
                        pltpu.VMEM((nq, NSA_DIM), F32), pltpu.VMEM((nq, NSA_DIM), F32)],
        compiler_params=_cparams(("arbitrary",)),
        name="nsa_sample",
    )(sel_idx, sel_idx, pt3, pt3, pool3, qn, kv_sel, kv_win, win4, o_cmp, gates, *consts)


MLA_PAGES_PER_STEP = 32
MLA_SAMPLE_STREAMS = 4


def _mla_sample_kernel(pt_ref, ptn_ref, pool_ref, q_ref, kn_ref, latn_ref, wukt_ref, wuv_ref, kg_ref, eye_ref,
                       o_ref, buf, sems, uq_sc, *stream_sc, layer, n_steps):
    c = pl.program_id(1)
    nq = q_ref.shape[1]
    nrow = MLA_HEADS * nq
    pages = MLA_PAGES_PER_STEP // MLA_SAMPLE_STREAMS
    tk = pages * PAGE_SIZE
    m_sc, l_sc, acc_sc = stream_sc[0::3], stream_sc[1::3], stream_sc[2::3]
    fetch = lambda pt, first, dst, sem, start: _fetch_pages(pool_ref, layer, pt, first, MLA_PAGES_PER_STEP, dst, sem, start)
    slot = _paged_pipeline(fetch, pt_ref, ptn_ref, buf, sems, n_steps, MLA_PAGES_PER_STEP)
    slot_of = lambda ref, h, lo, hi: ref[0, :, h * MLA_QK_PAD + lo:h * MLA_QK_PAD + hi]
    eye_n = eye_ref[0:nrow, 0:nrow].astype(F32)
    to_row = lambda col: jnp.sum(eye_n * col, axis=0, keepdims=True)

    def update(sts, ss, v_ts, valid=None):
        if valid is not None:
            ss = [jnp.where(valid, s, NEG_BIG) for s in ss]
        m_prev = [m_sc[st][...] for st in sts]
        m_new = [jnp.maximum(mp, jnp.max(s, axis=-1, keepdims=True)) for mp, s in zip(m_prev, ss)]
        ps = [jnp.exp(s - mn) for s, mn in zip(ss, m_new)]
        if valid is not None:
            ps = [jnp.where(valid, p, 0.0) for p in ps]
        alpha = [jnp.exp(mp - mn) for mp, mn in zip(m_prev, m_new)]
        pv = [_dot_nt(v_t, p.astype(BF16)) for v_t, p in zip(v_ts, ps)]
        for i, st in enumerate(sts):
            l_sc[st][...] = alpha[i] * l_sc[st][...] + jnp.sum(ps[i], axis=-1, keepdims=True)
            acc_sc[st][...] = to_row(alpha[i]) * acc_sc[st][...] + pv[i]
            m_sc[st][...] = m_new[i]

    @pl.when(c == 0)
    def _():
        for st in range(MLA_SAMPLE_STREAMS):
            _flash_init(m_sc[st], l_sc[st], acc_sc[st])
        us, rs = [], []
        for h in range(MLA_HEADS):
            qh = (slot_of(q_ref, h, 0, NOPE_DIM).astype(F32) * kg_ref[...]).astype(BF16)
            us.append(jnp.dot(qh, wukt_ref[h * NOPE_DIM:(h + 1) * NOPE_DIM, :], preferred_element_type=F32))
            rs.append(slot_of(q_ref, h, NOPE_DIM, NOPE_DIM + ROPE_DIM).astype(F32))
        top = jnp.concatenate([jnp.concatenate(us, axis=0), jnp.zeros((nrow, ROPE_DIM), F32)], axis=1)
        bot = jnp.concatenate([jnp.zeros((nrow, KV_LORA), F32), jnp.concatenate(rs, axis=0)], axis=1)
        uq_sc[...] = jnp.concatenate([top, bot], axis=0).astype(BF16)

    streams = range(MLA_SAMPLE_STREAMS)
    lat_t = [jnp.concatenate([buf[slot, st * pages + pg] for pg in range(pages)], axis=1).astype(BF16) for st in streams]
    c_t = [x[0:KV_LORA, :] for x in lat_t]
    knr = [jnp.dot(wukt_ref[...], x, preferred_element_type=F32) for x in c_t]
    raw = [jnp.dot(uq_sc[...], x, preferred_element_type=F32) for x in lat_t]
    inv = [lax.rsqrt(jnp.sum((x * x).reshape(MLA_HEADS, NOPE_DIM, tk), axis=1) * (1.0 / NOPE_DIM) + NORM_EPS) for x in knr]
    inv_rows = [jnp.concatenate([jnp.broadcast_to(x[h:h + 1, :], (nq, tk)) for h in range(MLA_HEADS)], axis=0) for x in inv]
    logits = [raw[st][0:nrow, :] * inv_rows[st] + raw[st][nrow:2 * nrow, :] for st in streams]
    update(list(streams), logits, c_t)

    @pl.when(c == n_steps - 1)
    def _():
        pad_rows = lambda x, n: jnp.concatenate([x, jnp.zeros((n - x.shape[0], x.shape[1]), x.dtype)], axis=0)
        s_new = jnp.concatenate([_dot_nt(slot_of(q_ref, h, 0, MLA_QK_PAD), pad_rows(slot_of(kn_ref, h, 0, MLA_QK_PAD), PAGE_SIZE))
                                 for h in range(MLA_HEADS)], axis=0)
        q_tok = lax.rem(lax.broadcasted_iota(jnp.int32, (nrow, PAGE_SIZE), 0), nq)
        key = lax.broadcasted_iota(jnp.int32, (nrow, PAGE_SIZE), 1)
        v_new_t = _dot_nt(eye_ref[...], pad_rows(latn_ref[0, :, 0:KV_LORA], PAGE_SIZE).astype(BF16)).astype(BF16)
        update([0], [s_new], [v_new_t], (key <= q_tok) & (key < nq))
        m_all = functools.reduce(jnp.maximum, [m_sc[st][...] for st in range(MLA_SAMPLE_STREAMS)])
        w = [jnp.exp(m_sc[st][...] - m_all) for st in range(MLA_SAMPLE_STREAMS)]
        l_all = sum(w[st] * l_sc[st][...] for st in range(MLA_SAMPLE_STREAMS))
        acc_t = sum(to_row(w[st]) * acc_sc[st][...] for st in range(MLA_SAMPLE_STREAMS))
        o_lat_t = (acc_t / to_row(jnp.maximum(l_all, 1e-30))).astype(BF16)
        o_lat = _dot_nt(eye_ref[0:nrow, 0:nrow], o_lat_t).astype(BF16)
        for h in range(MLA_HEADS):
            o_ref[0, :, h * V_DIM:(h + 1) * V_DIM] = jnp.dot(
                o_lat[h * nq:(h + 1) * nq, :], wuv_ref[:, h * V_DIM:(h + 1) * V_DIM],
                preferred_element_type=F32).astype(o_ref.dtype)


def mla_sample(pool_mla, layer, page_table, qm, km_new, lat_new, p):
    bsz, n_pages = page_table.shape
    nq = qm.shape[1]
    assert n_pages % MLA_PAGES_PER_STEP == 0
    n_steps = n_pages // MLA_PAGES_PER_STEP
    nrow = MLA_HEADS * nq
    wukt = p['mla_w_uk'].reshape(KV_LORA, MLA_HEADS * NOPE_DIM).T.astype(BF16)
    wuv = p['mla_w_uv'].reshape(KV_LORA, MLA_DIM).astype(BF16)
    kg = p['mla_k_nope_norm'].reshape(1, NOPE_DIM).astype(F32)
    eye = jnp.asarray(np.eye(PAGE_SIZE, dtype=np.float32), dtype=BF16)
    consts = [wukt, wuv, kg, eye]
    pt3 = page_table.reshape(bsz, 1, n_pages)
    width = KV_LORA + ROPE_DIM
    pool_t = _pool_feature_major(pool_mla)
    ns = MLA_SAMPLE_STREAMS
    assert MLA_PAGES_PER_STEP % ns == 0 and nrow <= PAGE_SIZE
    stream_scratch = [pltpu.VMEM((nrow, 1), F32), pltpu.VMEM((nrow, 1), F32), pltpu.VMEM((KV_LORA, nrow), F32)] * ns
    smem_pt = lambda off: pl.BlockSpec((1, 1, n_pages), lambda b, c: (jnp.minimum(b + off, bsz - 1), 0, 0),
                                       memory_space=pltpu.SMEM)
    tok = lambda w: pl.BlockSpec((1, nq, w), lambda b, c: (b, 0, 0))
    kern = functools.partial(_mla_sample_kernel, layer=layer, n_steps=n_steps)
    return pl.pallas_call(
        kern,
        out_shape=jax.ShapeDtypeStruct((bsz, nq, MLA_DIM), BF16),
        grid=(bsz, n_steps),
        in_specs=[smem_pt(0), smem_pt(1), pl.BlockSpec(memory_space=pl.ANY),
                  tok(qm.shape[2]), tok(km_new.shape[2]), tok(width)] + [_const_spec(x.shape) for x in consts],
        out_specs=tok(MLA_DIM),
        scratch_shapes=[pltpu.VMEM((2, MLA_PAGES_PER_STEP, width, PAGE_SIZE), F32), pltpu.SemaphoreType.DMA((2,)),
                        pltpu.VMEM((2 * nrow, width), BF16)] + stream_scratch,
        compiler_params=_cparams(("arbitrary", "arbitrary")),
        name="mla_sample",
    )(pt3, pt3, pool_t, qm, km_new, lat_new, *consts)


_LAYER_PARAMS = ('norm_mix', 'w_in', 'rw_mu', 'rw_w0', 'rw_w2', 'rw_a0', 'rw_a2', 'rw_g2', 'rw_k_k', 'rw_k_a', 'rw_r_k',
                 'rw_ln_w', 'rw_ln_b', 'nsa_q_norm', 'nsa_k_norm', 'cmp_pe', 'cmp_w1', 'cmp_w2', 'mla_q_a_norm',
                 'mla_w_uq', 'mla_kv_a_norm', 'mla_w_uk', 'mla_w_uv', 'mla_q_nope_norm', 'mla_q_rope_norm',
                 'mla_k_nope_norm', 'mla_k_rope_norm', 'w_br_a', 'w_br_b', 'w_br_c', 'w_out', 'norm_ffn',
                 'w_router_group', 'b_router_group', 'w_router_expert', 'b_router_expert')


def kernel(x_prompt, x_sample, cache_mla, cache_nsa_cmp, cache_nsa_sel, state_nsa_win, state_rwkv, state_rwkv_shift,
           page_table, rel_bias, norm_mix, w_in, rw_mu, rw_w0, rw_w2, rw_a0, rw_a2, rw_g2, rw_k_k, rw_k_a, rw_r_k,
           rw_ln_w, rw_ln_b, nsa_q_norm, nsa_k_norm, cmp_pe, cmp_w1, cmp_w2, mla_q_a_norm, mla_w_uq, mla_kv_a_norm,
           mla_w_uk, mla_w_uv, mla_q_nope_norm, mla_q_rope_norm, mla_k_nope_norm, mla_k_rope_norm, w_br_a, w_br_b,
           w_br_c, w_out, norm_ffn, w_router_group, b_router_group, w_router_expert, b_router_expert, w_gate, w_up,
           w_down):
    stacked = dict(zip(_LAYER_PARAMS, (norm_mix, w_in, rw_mu, rw_w0, rw_w2, rw_a0, rw_a2, rw_g2, rw_k_k, rw_k_a, rw_r_k,
                                       rw_ln_w, rw_ln_b, nsa_q_norm, nsa_k_norm, cmp_pe, cmp_w1, cmp_w2, mla_q_a_norm,
                                       mla_w_uq, mla_kv_a_norm, mla_w_uk, mla_w_uv, mla_q_nope_norm, mla_q_rope_norm,
                                       mla_k_nope_norm, mla_k_rope_norm, w_br_a, w_br_b, w_br_c, w_out, norm_ffn,
                                       w_router_group, b_router_group, w_router_expert, b_router_expert)))
    depth = w_in.shape[0]
    bp, seq, d_model = x_prompt.shape
    bs, nq, _ = x_sample.shape
    past = page_table.shape[1] * PAGE_SIZE
    xp = x_prompt.reshape(bp * seq, d_model)
    xs = x_sample.reshape(bs * nq, d_model)
    pos_p = jnp.tile(jnp.arange(seq, dtype=jnp.int32), bp)
    pos_s = jnp.tile(past + jnp.arange(nq, dtype=jnp.int32), bs)
    r3p = lambda a: a.reshape(bp, seq, a.shape[-1])
    r3s = lambda a: a.reshape(bs, nq, a.shape[-1])
    flat = lambda a: a.reshape(-1, a.shape[-1])
    kv_shape = (2, NSA_GROUPS, HEAD_DIM)
    new_p, new_s = [], []
    for l in range(depth):
        p = {k: v[l] for k, v in stacked.items()}
        w_inp, w_rw, w_cmp, w_mg = _prep_in_weights(p), _prep_rwkv_weights(p), _prep_cmp_weights(p), _prep_merge_weights(p)
        ip = in_proj(xp, pos_p, w_inp)
        iq = in_proj(xs, pos_s, w_inp)

        zrw_p, zrw_s = r3p(ip['zrw']), r3s(iq['zrw'])
        oa_p, st_p = rwkv_mix(zrw_p, jnp.zeros((bp, RWKV_COLS), F32), jnp.zeros((bp,) + state_rwkv.shape[2:], F32), w_rw)
        oa_s, st_s = rwkv_mix(zrw_s, state_rwkv_shift[l], state_rwkv[l], w_rw)

        ocmp_p, sel_p = nsa_cmp_prompt(r3p(ip['kvc']), r3p(ip['qn']), rel_bias, w_cmp)
        ob_p = nsa_prompt(r3p(ip['qn']), r3p(ip['kvs']), r3p(ip['kvw']), sel_p, ocmp_p, r3p(ip['gates']), rel_bias)
        oc_p = mla_prompt(r3p(ip['qm']), r3p(ip['km']), r3p(ip['lat']), p['mla_w_uv'])

        ocmp_s, idx_s = nsa_cmp_sample(cache_nsa_cmp, l, page_table, r3s(iq['qn']), rel_bias, w_cmp)
        ob_s, nwin_s = nsa_sample(cache_nsa_sel, state_nsa_win, l, page_table, idx_s, r3s(iq['qn']), r3s(iq['kvs']),
                                  r3s(iq['kvw']), ocmp_s, r3s(iq['gates']), rel_bias)
        oc_s = mla_sample(cache_mla, l, page_table, r3s(iq['qm']), r3s(iq['km']), r3s(iq['lat']), p)

        x1p, xnp, rt_p = merge_route(xp, flat(oa_p), flat(ob_p), flat(oc_p), ip['gm'], w_mg)
        xp = moe_ffn(x1p, xnp, rt_p, w_gate, w_up, w_down, l)
        x1s, xns, rt_s = merge_route(xs, flat(oa_s), flat(ob_s), flat(oc_s), iq['gm'], w_mg)
        xs = moe_ffn(x1s, xns, rt_s, w_gate, w_up, w_down, l)

        wkeep = min(WINDOW, seq)
        new_p.append((r3p(ip['lat']), r3p(ip['kvc']).reshape((bp, seq) + kv_shape),
                      r3p(ip['kvs']).reshape((bp, seq) + kv_shape),
                      r3p(ip['kvw'])[:, seq - wkeep:].reshape((bp, wkeep) + kv_shape), st_p, zrw_p[:, -1]))
        new_s.append((r3s(iq['lat']), r3s(iq['kvc']).reshape((bs, nq) + kv_shape),
                      r3s(iq['kvs']).reshape((bs, nq) + kv_shape),
                      nwin_s.reshape((bs, nwin_s.shape[1]) + kv_shape), st_s, zrw_s[:, -1]))
    stk = lambda states, i: jnp.stack([s[i] for s in states])
    return (xp.reshape(bp, seq, d_model), xs.reshape(bs, nq, d_model),
            stk(new_p, 0), stk(new_s, 0), stk(new_p, 1), stk(new_s, 1), stk(new_p, 2), stk(new_s, 2),
            stk(new_p, 3), stk(new_s, 3), stk(new_p, 4), stk(new_s, 4), stk(new_p, 5), stk(new_s, 5))
```
